```python
import numpy as np
import jax, jax.numpy as jnp
from jax import lax

D_MODEL = 1024
BATCH = 4
SEQ = 4096
DEPTH = 4

D_MIX = D_MODEL
D_LRU = D_MIX // 2
H_LRU = 8
LRU_BLK = D_LRU // H_LRU
CONV_W = 4
LRU_C = 8.0
D_NSA = D_MIX - D_LRU
HEAD_DIM = 64
N_HEADS = D_NSA // HEAD_DIM
N_KV = 2
GROUP = N_HEADS // N_KV
KV_W = N_KV * HEAD_DIM
CMP_LEN = 32
CMP_STRIDE = 16
CMP_HID = 256
SLC_BLK = 64
SLC_TOPK = 16
WINDOW = 512
WIN_QBLK = 128
SLC_QBLK = 64
ROPE_THETA = 10000.0
EPS = 1e-6
NEG = -1e30
FORCE_BONUS = 1e4
SCALE = HEAD_DIM ** -0.5
D_IN = 2 * D_LRU + 2 * D_NSA + 6 * KV_W + 3 * N_HEADS

kernel_name = 'hymba_griffin_nsa_hybrid'

F32 = jnp.float32


def rmsnorm(x, g):
    xf = x.astype(F32)
    y = xf * lax.rsqrt(jnp.mean(xf * xf, axis=-1, keepdims=True) + EPS)
    return (y * g.astype(F32)).astype(x.dtype)


def rope_tables(seq):
    inv = 1.0 / (ROPE_THETA ** (jnp.arange(0, HEAD_DIM, 2, dtype=F32) / HEAD_DIM))
    ang = jnp.arange(seq, dtype=F32)[:, None] * inv[None, :]
    return jnp.cos(ang), jnp.sin(ang)


def apply_rope(t, cos, sin):
    x1, x2 = jnp.split(t.astype(F32), 2, axis=-1)
    c = cos[None, :, None, :]
    s = sin[None, :, None, :]
    return jnp.concatenate([x1 * c - x2 * s, x2 * c + x1 * s], axis=-1).astype(t.dtype)


def causal_depthwise_conv(u, w, b):
    out = lax.conv_general_dilated(u, w[:, None, :].astype(u.dtype), window_strides=(1,), padding=[(CONV_W - 1, 0)], dimension_numbers=('NWC', 'WIO', 'NWC'), feature_group_count=u.shape[-1])
    return out + b


def rg_lru(u, wa, ba, wi, bi, lam):
    B, S, _ = u.shape
    uf = u.astype(F32)
    uh = uf.reshape(B, S, H_LRU, LRU_BLK)
    r = jax.nn.sigmoid(jnp.einsum('bshi,hij->bshj', uh, wa.astype(F32)).reshape(B, S, D_LRU) + ba.astype(F32))
    i = jax.nn.sigmoid(jnp.einsum('bshi,hij->bshj', uh, wi.astype(F32)).reshape(B, S, D_LRU) + bi.astype(F32))
    log_a = -LRU_C * r * jax.nn.softplus(-lam.astype(F32))
    a = jnp.exp(log_a)
    drive = jnp.sqrt(-jnp.expm1(2.0 * log_a)) * (i * uf)

    def combine(lhs, rhs):
        a1, b1 = lhs
        a2, b2 = rhs
        return a1 * a2, a2 * b1 + b2

    _, h = lax.associative_scan(combine, (a, drive), axis=1)
    return h


def compress(kv, pos, w1, w2):
    B, S, Hk, dh = kv.shape
    n_sub = S // CMP_STRIDE
    ratio = CMP_LEN // CMP_STRIDE
    nc = n_sub - ratio + 1
    sub = kv.reshape(B, n_sub, CMP_STRIDE, Hk, dh)
    blocks = jnp.concatenate([sub[:, r:r + nc] for r in range(ratio)], axis=2)
    blocks = blocks + pos[None, None, :, None, :]
    flat = jnp.moveaxis(blocks, 3, 2).reshape(B, nc, Hk, CMP_LEN * dh)
    return jax.nn.silu(flat @ w1) @ w2


def cmp_attention(q, kc, vc):
    B, S, H, dh = q.shape
    nc = kc.shape[1]
    qg = q.reshape(B, S, N_KV, GROUP, dh)
    s = jnp.einsum('bskgd,bckd->bkgsc', qg, kc, preferred_element_type=F32) * SCALE
    t = jnp.arange(S)
    blk_end = jnp.arange(nc) * CMP_STRIDE + CMP_LEN - 1
    valid = blk_end[None, :] <= t[:, None]
    s = jnp.where(valid, s, NEG)
    p = jnp.where(valid, jax.nn.softmax(s, axis=-1), 0.0)
    o = jnp.einsum('bkgsc,bckd->bskgd', p, vc.astype(F32)).reshape(B, S, H, dh)
    return o, p


def selection_indices(p_cmp, seq):
    nc = p_cmp.shape[-1]
    n_slc = seq // SLC_BLK
    c0 = np.arange(nc)[:, None] * CMP_STRIDE
    j0 = np.arange(n_slc)[None, :] * SLC_BLK
    overlap = np.clip(np.minimum(c0 + CMP_LEN, j0 + SLC_BLK) - np.maximum(c0, j0), 0, None) / CMP_LEN
    m = jnp.asarray(overlap, dtype=F32)
    p_slc = jnp.einsum('bkgsc,cj->bksj', p_cmp, m)
    cur = (jnp.arange(seq) // SLC_BLK)[:, None]
    jj = jnp.arange(n_slc)[None, :]
    forced = (jj == 0) | (jj == cur) | (jj == cur - 1)
    score = jnp.where(forced, p_slc + FORCE_BONUS, p_slc)
    score = jnp.where(jj <= cur, score, NEG)
    k = min(SLC_TOPK, n_slc)
    top_s, idx = lax.top_k(score, k)
    return idx, top_s > NEG / 2


def slc_attention(q, k, v, idx, ok):
    B, S, H, dh = q.shape
    n_slc = S // SLC_BLK
    kk = idx.shape[-1]
    nq = S // SLC_QBLK
    kb = jnp.moveaxis(k.reshape(B, n_slc, SLC_BLK, N_KV, dh), 3, 1)
    vb = jnp.moveaxis(v.reshape(B, n_slc, SLC_BLK, N_KV, dh), 3, 1)
    qc = jnp.moveaxis(q.reshape(B, nq, SLC_QBLK, N_KV, GROUP, dh), 1, 0)
    ic = jnp.moveaxis(idx.reshape(B, N_KV, nq, SLC_QBLK, kk), 2, 0)
    okc = jnp.moveaxis(ok.reshape(B, N_KV, nq, SLC_QBLK, kk), 2, 0)
    tc = jnp.arange(S).reshape(nq, SLC_QBLK)
    gather = jax.vmap(jax.vmap(lambda blocks, ix: blocks[ix]))

    def one_block(args):
        qb, ib, okb, tb = args
        ks = gather(kb, ib)
        vs = gather(vb, ib)
        s = jnp.einsum('bqkgd,bkqnpd->bkgqnp', qb, ks, preferred_element_type=F32) * SCALE
        tok = ib[..., None] * SLC_BLK + jnp.arange(SLC_BLK)
        mask = okb[..., None] & (tok <= tb[None, None, :, None, None])
        s = jnp.where(mask[:, :, None], s, NEG)
        p = jax.nn.softmax(s.reshape(B, N_KV, GROUP, SLC_QBLK, kk * SLC_BLK), axis=-1).reshape(s.shape)
        return jnp.einsum('bkgqnp,bkqnpd->bqkgd', p, vs.astype(F32))

    out = lax.map(one_block, (qc, ic, okc, tc))
    return jnp.moveaxis(out, 0, 1).reshape(B, S, H, dh)


def win_attention(q, k, v):
    B, S, H, dh = q.shape
    nb = S // WIN_QBLK
    nback = WINDOW // WIN_QBLK
    pad = ((0, 0), (WINDOW, 0), (0, 0), (0, 0))
    kp = jnp.pad(k, pad).reshape(B, nb + nback, WIN_QBLK, N_KV, dh)
    vp = jnp.pad(v, pad).reshape(B, nb + nback, WIN_QBLK, N_KV, dh)
    kw = jnp.concatenate([kp[:, j:j + nb] for j in range(nback + 1)], axis=2)
    vw = jnp.concatenate([vp[:, j:j + nb] for j in range(nback + 1)], axis=2)
    qb = q.reshape(B, nb, WIN_QBLK, N_KV, GROUP, dh)
    s = jnp.einsum('bnqkgd,bnpkd->bnkgqp', qb, kw, preferred_element_type=F32) * SCALE
    n_idx = jnp.arange(nb)[:, None]
    tq = n_idx * WIN_QBLK + jnp.arange(WIN_QBLK)[None, :]
    tk = n_idx * WIN_QBLK - WINDOW + jnp.arange((nback + 1) * WIN_QBLK)[None, :]
    diff = tq[:, :, None] - tk[:, None, :]
    mask = (diff >= 0) & (diff < WINDOW) & (tk[:, None, :] >= 0)
    s = jnp.where(mask[None, :, None, None], s, NEG)
    p = jax.nn.softmax(s, axis=-1)
    return jnp.einsum('bnkgqp,bnpkd->bnqkgd', p, vw.astype(F32)).reshape(B, S, H, dh)


def hybrid_layer(x, g, w_in, conv_w, conv_b, wa, ba, wi, bi, lam, pos_k, pos_v, kw1, kw2, vw1, vw2, w_out, cos, sin):
    B, S, _ = x.shape
    h = rmsnorm(x, g)
    z = h @ w_in
    splits = [int(c) for c in np.cumsum([D_LRU, D_LRU, D_NSA, D_NSA, KV_W, KV_W, KV_W, KV_W, KV_W, KV_W])]
    u_lru, g_lru, q, g_nsa, kc, vc, ks, vs, kwn, vwn, gate_logits = jnp.split(z, splits, axis=-1)
    y_lru = rg_lru(causal_depthwise_conv(u_lru, conv_w, conv_b), wa, ba, wi, bi, lam)
    heads = lambda t: t.reshape(B, S, N_KV, HEAD_DIM)
    q = q.reshape(B, S, N_HEADS, HEAD_DIM)
    o_cmp, p_cmp = cmp_attention(q, compress(heads(kc), pos_k, kw1, kw2), compress(heads(vc), pos_v, vw1, vw2))
    idx, ok = selection_indices(p_cmp, S)
    q_rot = apply_rope(q, cos, sin)
    o_slc = slc_attention(q_rot, apply_rope(heads(ks), cos, sin), heads(vs), idx, ok)
    o_win = win_attention(q_rot, apply_rope(heads(kwn), cos, sin), heads(vwn))
    gates = jax.nn.sigmoid(gate_logits.astype(F32)).reshape(B, S, N_HEADS, 3)
    y_nsa = (gates[..., 0:1] * o_cmp + gates[..., 1:2] * o_slc + gates[..., 2:3] * o_win).reshape(B, S, D_NSA)
    y = jnp.concatenate([y_lru * jax.nn.silu(g_lru.astype(F32)), y_nsa * jax.nn.silu(g_nsa.astype(F32))], axis=-1).astype(x.dtype)
    return x + y @ w_out


def setup_inputs(seed: int = 0) -> dict:
    key = jax.random.key(seed)
    ks = jax.random.split(key, 20)
    nrm = lambda k, shape, scale: jax.random.normal(k, shape, dtype=F32) * scale
    u = jax.random.uniform(ks[7], (DEPTH, D_LRU), dtype=F32, minval=0.9, maxval=0.999)
    a0 = u ** (1.0 / LRU_C)
    return {
        'x': nrm(ks[0], (BATCH, SEQ, D_MODEL), 1.0),
        'norm_g': 1.0 + nrm(ks[1], (DEPTH, D_MODEL), 0.02),
        'w_in': nrm(ks[2], (DEPTH, D_MODEL, D_IN), D_MODEL ** -0.5),
        'conv_w': nrm(ks[3], (DEPTH, CONV_W, D_LRU), CONV_W ** -0.5),
        'conv_b': nrm(ks[4], (DEPTH, D_LRU), 0.02),
        'lru_wa': nrm(ks[5], (DEPTH, H_LRU, LRU_BLK, LRU_BLK), LRU_BLK ** -0.5),
        'lru_ba': nrm(ks[6], (DEPTH, D_LRU), 0.02),
        'lru_wi': nrm(ks[8], (DEPTH, H_LRU, LRU_BLK, LRU_BLK), LRU_BLK ** -0.5),
        'lru_bi': nrm(ks[9], (DEPTH, D_LRU), 0.02),
        'lru_lambda': jnp.log(a0) - jnp.log1p(-a0),
        'cmp_pos_k': nrm(ks[10], (DEPTH, CMP_LEN, HEAD_DIM), 0.1),
        'cmp_pos_v': nrm(ks[11], (DEPTH, CMP_LEN, HEAD_DIM), 0.1),
        'cmp_k_w1': nrm(ks[12], (DEPTH, CMP_LEN * HEAD_DIM, CMP_HID), (CMP_LEN * HEAD_DIM) ** -0.5),
        'cmp_k_w2': nrm(ks[13], (DEPTH, CMP_HID, HEAD_DIM), CMP_HID ** -0.5),
        'cmp_v_w1': nrm(ks[14], (DEPTH, CMP_LEN * HEAD_DIM, CMP_HID), (CMP_LEN * HEAD_DIM) ** -0.5),
        'cmp_v_w2': nrm(ks[15], (DEPTH, CMP_HID, HEAD_DIM), CMP_HID ** -0.5),
        'w_out': nrm(ks[16], (DEPTH, D_MIX, D_MODEL), D_MIX ** -0.5),
        'final_g': 1.0 + nrm(ks[17], (D_MODEL,), 0.02),
    }


def reference(x, norm_g, w_in, conv_w, conv_b, lru_wa, lru_ba, lru_wi, lru_bi, lru_lambda, cmp_pos_k, cmp_pos_v, cmp_k_w1, cmp_k_w2, cmp_v_w1, cmp_v_w2, w_out, final_g):
    cos, sin = rope_tables(x.shape[1])
    h = x
    for l in range(DEPTH):
        h = hybrid_layer(h, norm_g[l], w_in[l], conv_w[l], conv_b[l], lru_wa[l], lru_ba[l], lru_wi[l], lru_bi[l], lru_lambda[l], cmp_pos_k[l], cmp_pos_v[l], cmp_k_w1[l], cmp_k_w2[l], cmp_v_w1[l], cmp_v_w2[l], w_out[l], cos, sin)
    return rmsnorm(h, final_g)
```

```python
import functools

import numpy as np
import jax
import jax.numpy as jnp
from jax import lax
from jax.experimental import pallas as pl
from jax.experimental.pallas import tpu as pltpu

F32 = jnp.float32
BF16 = jnp.bfloat16

D_MODEL = 1024
D_LRU = 512
H_LRU = 8
LRU_BLK = D_LRU // H_LRU
CONV_W = 4
LRU_C = 8.0
D_NSA = 512
HEAD_DIM = 64
N_HEADS = 8
N_KV = 2
GROUP = N_HEADS // N_KV
KV_W = N_KV * HEAD_DIM
CMP_LEN = 32
CMP_STRIDE = 16
CMP_HID = 256
SLC_BLK = 64
SLC_TOPK = 16
WINDOW = 512
WIN_QBLK = 128
ROPE_THETA = 10000.0
EPS = 1e-6
NEG = -1e30
FORCE_BONUS = 1e4
SCALE = HEAD_DIM ** -0.5
D_IN = 2 * D_LRU + 2 * D_NSA + 6 * KV_W + 3 * N_HEADS
LANE = 128
D_IN_PAD = ((D_IN + LANE - 1) // LANE) * LANE
SUB_FLAT = CMP_STRIDE * HEAD_DIM

OFF_U = 0
OFF_GL = D_LRU
OFF_Q = 2 * D_LRU
OFF_GN = 2 * D_LRU + D_NSA
OFF_KV = 2 * D_LRU + 2 * D_NSA
OFF_GATE = OFF_KV + 6 * KV_W

ROW_TILE = 512
LRU_TILE = 256
CMP_QTILE = 256
SLC_CHUNK = 256
VMEM_LIMIT = 48 * 1024 * 1024

_NT = (((1,), (1,)), ((), ()))


def _params(sem):
    return pltpu.CompilerParams(dimension_semantics=sem, vmem_limit_bytes=VMEM_LIMIT)


def _rope(t, cos, sin_signed):
    lane = lax.broadcasted_iota(jnp.int32, t.shape, 1)
    first = (lane % HEAD_DIM) < (HEAD_DIM // 2)
    rot = jnp.where(first, pltpu.roll(t, LANE - HEAD_DIM // 2, 1), pltpu.roll(t, HEAD_DIM // 2, 1))
    return t * cos + rot * sin_signed


def _inproj_kernel(x_ref, g_ref, w_ref, cos_ref, sin_ref,
                   u_ref, gl_ref, gn_ref, gate_ref, kc_ref, vc_ref,
                   qraw_ref, qrot_ref, ks_ref, vs_ref, kw_ref, vw_ref):
    x = x_ref[...]
    ms = jnp.mean(x * x, axis=-1, keepdims=True)
    h = ((x * lax.rsqrt(ms + EPS)) * g_ref[...]).astype(BF16)
    cos = cos_ref[...]
    sin = sin_ref[...]

    def proj(off, width):
        return jnp.dot(h, w_ref[:, off:off + width], preferred_element_type=F32)

    u_ref[...] = proj(OFF_U, D_LRU)
    gl_ref[...] = proj(OFF_GL, D_LRU)
    gn_ref[...] = proj(OFF_GN, D_NSA)
    gate_ref[...] = proj(OFF_GATE, LANE)
    kc_ref[...] = proj(OFF_KV, KV_W)
    vc_ref[...] = proj(OFF_KV + KV_W, KV_W)

    def put_heads(ref, base, slab):
        ref[base] = slab[:, :HEAD_DIM].astype(ref.dtype)
        ref[base + 1] = slab[:, HEAD_DIM:].astype(ref.dtype)

    for i in range(D_NSA // LANE):
        slab = proj(OFF_Q + i * LANE, LANE)
        put_heads(qraw_ref, 2 * i, slab)
        put_heads(qrot_ref, 2 * i, _rope(slab, cos, sin))
    put_heads(ks_ref, 0, _rope(proj(OFF_KV + 2 * KV_W, KV_W), cos, sin))
    put_heads(vs_ref, 0, proj(OFF_KV + 3 * KV_W, KV_W))
    put_heads(kw_ref, 0, _rope(proj(OFF_KV + 4 * KV_W, KV_W), cos, sin))
    put_heads(vw_ref, 0, proj(OFF_KV + 5 * KV_W, KV_W))


def _inproj(x2, g, w, cosf, sinf, seq):
    m = x2.shape[0]
    tm = ROW_TILE
    spt = seq // tm
    row = lambda width: pl.BlockSpec((tm, width), lambda i: (i, 0))
    heads = lambda n: pl.BlockSpec((n, tm, HEAD_DIM), lambda i: (0, i, 0))
    tab = pl.BlockSpec((tm, LANE), lambda i: (i % spt, 0))
    out_shape = (
        jax.ShapeDtypeStruct((m, D_LRU), F32),
        jax.ShapeDtypeStruct((m, D_LRU), F32),
        jax.ShapeDtypeStruct((m, D_NSA), F32),
        jax.ShapeDtypeStruct((m, LANE), F32),
        jax.ShapeDtypeStruct((m, KV_W), F32),
        jax.ShapeDtypeStruct((m, KV_W), F32),
        jax.ShapeDtypeStruct((N_HEADS, m, HEAD_DIM), BF16),
        jax.ShapeDtypeStruct((N_HEADS, m, HEAD_DIM), BF16),
        jax.ShapeDtypeStruct((N_KV, m, HEAD_DIM), BF16),
        jax.ShapeDtypeStruct((N_KV, m, HEAD_DIM), BF16),
        jax.ShapeDtypeStruct((N_KV, m, HEAD_DIM), BF16),
        jax.ShapeDtypeStruct((N_KV, m, HEAD_DIM), BF16),
    )
    out_specs = (row(D_LRU), row(D_LRU), row(D_NSA), row(LANE), row(KV_W), row(KV_W),
                 heads(N_HEADS), heads(N_HEADS), heads(N_KV), heads(N_KV), heads(N_KV), heads(N_KV))
    return pl.pallas_call(
        _inproj_kernel,
        grid=(m // tm,),
        in_specs=[row(D_MODEL),
                  pl.BlockSpec((1, D_MODEL), lambda i: (0, 0)),
                  pl.BlockSpec((D_MODEL, D_IN_PAD), lambda i: (0, 0)),
                  tab, tab],
        out_specs=out_specs,
        out_shape=out_shape,
        compiler_params=_params(("parallel",)),
        name="inproj",
    )(x2, g, w, cosf, sinf)


def _softplus(x):
    return jnp.maximum(x, 0.0) + jnp.log1p(jnp.exp(-jnp.abs(x)))


def _lru_kernel(u_ref, g_ref, cw_ref, cb_ref, wg_ref, ba_ref, bi_ref, lam_ref, y_ref,
                ubuf, hprev, a0, b0, a1, b1):
    t = u_ref.shape[0]
    pad = t // 2
    head = 8

    @pl.when(pl.program_id(1) == 0)
    def _():
        ubuf[0:head, :] = jnp.zeros((head, D_LRU), F32)
        hprev[...] = jnp.zeros_like(hprev)

    ubuf[head:head + t, :] = u_ref[...]
    conv = cb_ref[...]
    for k in range(CONV_W):
        conv = conv + cw_ref[k:k + 1, :] * ubuf[pl.ds(head - (CONV_W - 1) + k, t), :]
    ubuf[head - (CONV_W - 1):head, :] = ubuf[head + t - (CONV_W - 1):head + t, :]

    pre = jnp.dot(conv.astype(BF16), wg_ref[...], preferred_element_type=F32)
    r = jax.nn.sigmoid(pre[:, :D_LRU] + ba_ref[...])
    gi = jax.nn.sigmoid(pre[:, D_LRU:] + bi_ref[...])
    log_a = (-LRU_C * r) * _softplus(-lam_ref[...])
    a = jnp.exp(log_a)
    drive = jnp.sqrt((1.0 - a) * (1.0 + a)) * (gi * conv)

    bufs = ((a0, b0), (a1, b1))
    for ab, bb in bufs:
        ab[0:pad, :] = jnp.ones((pad, D_LRU), F32)
        bb[0:pad, :] = jnp.zeros((pad, D_LRU), F32)
    a0[pad:pad + t, :] = a
    b0[pad:pad + t, :] = drive
    src = 0
    d = 1
    while d < t:
        sa, sb = bufs[src]
        da, db = bufs[1 - src]
        a_cur = sa[pad:pad + t, :]
        db[pad:pad + t, :] = sb[pad:pad + t, :] + a_cur * sb[pl.ds(pad - d, t), :]
        da[pad:pad + t, :] = a_cur * sa[pl.ds(pad - d, t), :]
        src = 1 - src
        d *= 2
    fa, fb = bufs[src]
    hseq = fa[pad:pad + t, :] * hprev[...] + fb[pad:pad + t, :]
    hprev[...] = hseq[t - 1:t, :]
    y_ref[...] = (hseq * jax.nn.silu(g_ref[...])).astype(y_ref.dtype)


def _lru(u, gl, cw, cb, wg, ba, bi, lam, batch, seq):
    t = LRU_TILE
    spt = seq // t
    row = pl.BlockSpec((t, D_LRU), lambda b, j: (b * spt + j, 0))
    vec = pl.BlockSpec((1, D_LRU), lambda b, j: (0, 0))
    pad = t // 2
    return pl.pallas_call(
        _lru_kernel,
        grid=(batch, spt),
        in_specs=[row, row,
                  pl.BlockSpec((CONV_W, D_LRU), lambda b, j: (0, 0)), vec,
                  pl.BlockSpec((D_LRU, 2 * D_LRU), lambda b, j: (0, 0)), vec, vec, vec],
        out_specs=row,
        out_shape=jax.ShapeDtypeStruct(u.shape, BF16),
        scratch_shapes=[pltpu.VMEM((8 + t, D_LRU), F32), pltpu.VMEM((1, D_LRU), F32)]
        + [pltpu.VMEM((pad + t, D_LRU), F32)] * 4,
        compiler_params=_params(("parallel", "arbitrary")),
        name="rglru",
    )(u, gl, cw, cb, wg, ba, bi, lam)


def _compress_kernel(k_ref, v_ref, pk_ref, pv_ref, k1_ref, k2_ref, v1_ref, v2_ref, ko_ref, vo_ref, shift):
    n = k_ref.shape[2]

    def one(x_ref, p_ref, w1_ref, w2_ref, o_ref):
        x = x_ref[0, 0]
        lo = jnp.dot((x + p_ref[0:1, :]).astype(BF16), w1_ref[0], preferred_element_type=F32)
        hi = jnp.dot((x + p_ref[1:2, :]).astype(BF16), w1_ref[1], preferred_element_type=F32)
        shift[0:n, :] = hi
        shift[n:n + 8, :] = jnp.zeros((8, CMP_HID), F32)
        pre = lo + shift[pl.ds(1, n), :]
        o = jnp.dot(jax.nn.silu(pre).astype(BF16), w2_ref[...], preferred_element_type=F32)
        o_ref[0, 0] = o.astype(o_ref.dtype)

    one(k_ref, pk_ref, k1_ref, k2_ref, ko_ref)
    one(v_ref, pv_ref, v1_ref, v2_ref, vo_ref)


def _compress(kc, vc, pk, pv, k1, k2, v1, v2, batch, seq):
    n = seq // CMP_STRIDE

    def relayout(t):
        t = t.reshape(batch, n, CMP_STRIDE, N_KV, HEAD_DIM)
        return jnp.transpose(t, (0, 3, 1, 2, 4)).reshape(batch, N_KV, n, SUB_FLAT)

    xin = pl.BlockSpec((1, 1, n, SUB_FLAT), lambda b, k: (b, k, 0, 0))
    pos = pl.BlockSpec((2, SUB_FLAT), lambda b, k: (0, 0))
    w1 = pl.BlockSpec((2, SUB_FLAT, CMP_HID), lambda b, k: (0, 0, 0))
    w2 = pl.BlockSpec((CMP_HID, HEAD_DIM), lambda b, k: (0, 0))
    out = pl.BlockSpec((1, 1, n, HEAD_DIM), lambda b, k: (b, k, 0, 0))
    oshape = jax.ShapeDtypeStruct((batch, N_KV, n, HEAD_DIM), BF16)
    return pl.pallas_call(
        _compress_kernel,
        grid=(batch, N_KV),
        in_specs=[xin, xin, pos, pos, w1, w2, w1, w2],
        out_specs=(out, out),
        out_shape=(oshape, oshape),
        scratch_shapes=[pltpu.VMEM((n + 8, CMP_HID), F32)],
        compiler_params=_params(("parallel", "parallel")),
        name="compress",
    )(relayout(kc), relayout(vc), pk, pv, k1, k2, v1, v2)


def _split3(x):
    hi = x.astype(BF16)
    r1 = x - hi.astype(F32)
    mid = r1.astype(BF16)
    lo = (r1 - mid.astype(F32)).astype(BF16)
    return hi, mid, lo


def _cmpsel_kernel(q_ref, kc_ref, vc_ref, mt_ref, o_ref, sel_ref, sc_ref):
    tq = q_ref.shape[1]
    ncp = kc_ref.shape[2]
    nb = mt_ref.shape[0]
    t0 = pl.program_id(2) * tq
    kc = kc_ref[0, 0]
    vc = vc_ref[0, 0]
    tok = t0 + lax.broadcasted_iota(jnp.int32, (tq, ncp), 0)
    blk_end = lax.broadcasted_iota(jnp.int32, (tq, ncp), 1) * CMP_STRIDE + (CMP_LEN - 1)
    valid = blk_end <= tok
    psum = jnp.zeros((tq, ncp), F32)
    for g in range(GROUP):
        s = lax.dot_general(q_ref[g], kc, _NT, preferred_element_type=F32) * SCALE
        s = jnp.where(valid, s, NEG)
        e = jnp.exp(s - jnp.max(s, axis=-1, keepdims=True))
        p = jnp.where(valid, e / jnp.sum(e, axis=-1, keepdims=True), 0.0)
        o_ref[g] = jnp.dot(p.astype(BF16), vc, preferred_element_type=F32)
        psum = psum + p

    mt = mt_ref[...]
    sc = jnp.zeros((nb, tq), F32)
    for part in _split3(psum):
        sc = sc + lax.dot_general(mt, part, _NT, preferred_element_type=F32)
    jj = lax.broadcasted_iota(jnp.int32, (nb, tq), 0)
    cur = (t0 + lax.broadcasted_iota(jnp.int32, (nb, tq), 1)) // SLC_BLK
    forced = (jj == 0) | (jj == cur) | (jj == cur - 1)
    sc = jnp.where(forced, sc + FORCE_BONUS, sc)
    sc = jnp.where(jj <= cur, sc, NEG)
    sc_ref[...] = sc

    def body(jp, cnt):
        r = sc_ref[pl.ds(jp, 1), :]
        before = (r > sc) | ((r == sc) & (jp < jj))
        return cnt + before.astype(jnp.int32)

    cnt = lax.fori_loop(0, nb, body, jnp.zeros((nb, tq), jnp.int32), unroll=8)
    sel_t = ((cnt < min(SLC_TOPK, nb)) & (sc > NEG / 2)).astype(BF16)
    eye = (lax.broadcasted_iota(jnp.int32, (tq, tq), 0)
           == lax.broadcasted_iota(jnp.int32, (tq, tq), 1)).astype(BF16)
    sel_ref[0, 0] = lax.dot_general(eye, sel_t, _NT, preferred_element_type=F32).astype(sel_ref.dtype)


def _overlap_t(seq):
    nc_pad = seq // CMP_STRIDE
    n_slc = seq // SLC_BLK
    c0 = np.arange(nc_pad)[None, :] * CMP_STRIDE
    j0 = np.arange(n_slc)[:, None] * SLC_BLK
    ov = np.clip(np.minimum(c0 + CMP_LEN, j0 + SLC_BLK) - np.maximum(c0, j0), 0, None) / CMP_LEN
    ov[:, nc_pad - 1] = 0.0
    return jnp.asarray(ov, dtype=BF16)


def _cmpsel(qraw, kcmp, vcmp, batch, seq):
    tq = CMP_QTILE
    spt = seq // tq
    ncp = seq // CMP_STRIDE
    nb = seq // SLC_BLK
    m = batch * seq
    qspec = pl.BlockSpec((GROUP, tq, HEAD_DIM), lambda b, k, i: (k, b * spt + i, 0))
    cspec = pl.BlockSpec((1, 1, ncp, HEAD_DIM), lambda b, k, i: (b, k, 0, 0))
    return pl.pallas_call(
        _cmpsel_kernel,
        grid=(batch, N_KV, spt),
        in_specs=[qspec, cspec, cspec, pl.BlockSpec((nb, ncp), lambda b, k, i: (0, 0))],
        out_specs=(qspec, pl.BlockSpec((1, 1, tq, nb), lambda b, k, i: (b, k, i, 0))),
        out_shape=(jax.ShapeDtypeStruct((N_HEADS, m, HEAD_DIM), F32),
                   jax.ShapeDtypeStruct((batch, N_KV, seq, nb), BF16)),
        scratch_shapes=[pltpu.VMEM((nb, tq), F32)],
        compiler_params=_params(("parallel", "parallel", "parallel")),
        name="cmpsel",
    )(qraw, kcmp, vcmp, _overlap_t(seq))


def _slc_kernel(q_ref, k_ref, v_ref, sel_ref, o_ref):
    tq = q_ref.shape[1]
    nb = sel_ref.shape[3]
    rows = GROUP * tq
    ck = SLC_CHUNK
    bpc = ck // SLC_BLK
    qi = pl.program_id(2)
    q = q_ref[...].reshape(rows, HEAD_DIM)
    sel = sel_ref[0, 0]
    tq_tok = qi * tq + lax.broadcasted_iota(jnp.int32, (tq, ck), 0)
    col = lax.broadcasted_iota(jnp.int32, (tq, ck), 1)
    erow = lax.broadcasted_iota(jnp.int32, (nb, ck), 0)
    ecol_blk = lax.broadcasted_iota(jnp.int32, (nb, ck), 1) // SLC_BLK

    def body(c, carry):
        m_i, l_i, acc = carry
        start = pl.multiple_of(c * ck, ck)
        kc = k_ref[0, pl.ds(start, ck), :]
        vc = v_ref[0, pl.ds(start, ck), :]
        s = lax.dot_general(q, kc, _NT, preferred_element_type=F32) * SCALE
        expand = (erow == c * bpc + ecol_blk).astype(BF16)
        allow = jnp.dot(sel, expand, preferred_element_type=F32)
        mask1 = (allow > 0.5) & (start + col <= tq_tok)
        mask = jnp.concatenate([mask1] * GROUP, axis=0)
        s = jnp.where(mask, s, NEG)
        m_new = jnp.maximum(m_i, jnp.max(s, axis=-1, keepdims=True))
        p = jnp.where(mask, jnp.exp(s - m_new), 0.0)
        alpha = jnp.exp(m_i - m_new)
        l_new = alpha * l_i + jnp.sum(p, axis=-1, keepdims=True)
        acc = alpha * acc + jnp.dot(p.astype(BF16), vc, preferred_element_type=F32)
        return m_new, l_new, acc

    n_chunks = qi // bpc + 1
    init = (jnp.full((rows, 1), NEG, F32), jnp.zeros((rows, 1), F32), jnp.zeros((rows, HEAD_DIM), F32))
    _, l_i, acc = lax.fori_loop(0, n_chunks, body, init)
    o_ref[...] = (acc / l_i).reshape(GROUP, tq, HEAD_DIM)


def _slc(qrot, ks, vs, sel, batch, seq):
    tq = SLC_BLK
    nq = seq // tq
    nb = seq // SLC_BLK
    qspec = pl.BlockSpec((GROUP, tq, HEAD_DIM), lambda b, k, i: (k, b * nq + i, 0))
    kvspec = pl.BlockSpec((1, seq, HEAD_DIM), lambda b, k, i: (k, b, 0))
    return pl.pallas_call(
        _slc_kernel,
        grid=(batch, N_KV, nq),
        in_specs=[qspec, kvspec, kvspec, pl.BlockSpec((1, 1, tq, nb), lambda b, k, i: (b, k, i, 0))],
        out_specs=qspec,
        out_shape=jax.ShapeDtypeStruct((N_HEADS, batch * seq, HEAD_DIM), F32),
        compiler_params=_params(("parallel", "parallel", "parallel")),
        name="slcattn",
    )(qrot, ks, vs, sel)


def _win_kernel(q_ref, k_ref, v_ref, o_ref):
    tq = q_ref.shape[1]
    rows = GROUP * tq
    span = WINDOW + tq
    n = pl.program_id(2)
    q = q_ref[...].reshape(rows, HEAD_DIM)
    start = pl.multiple_of(jnp.maximum(n - WINDOW // tq, 0) * tq, tq)
    kw = k_ref[0, pl.ds(start, span), :]
    vw = v_ref[0, pl.ds(start, span), :]
    s = lax.dot_general(q, kw, _NT, preferred_element_type=F32) * SCALE
    tok = n * tq + lax.broadcasted_iota(jnp.int32, (tq, span), 0)
    key = start + lax.broadcasted_iota(jnp.int32, (tq, span), 1)
    diff = tok - key
    mask1 = (diff >= 0) & (diff < WINDOW)
    mask = jnp.concatenate([mask1] * GROUP, axis=0)
    s = jnp.where(mask, s, NEG)
    e = jnp.exp(s - jnp.max(s, axis=-1, keepdims=True))
    p = e / jnp.sum(e, axis=-1, keepdims=True)
    o = jnp.dot(p.astype(BF16), vw, preferred_element_type=F32)
    o_ref[...] = o.reshape(GROUP, tq, HEAD_DIM)


def _win(qrot, kw, vw, batch, seq):
    tq = WIN_QBLK
    nq = seq // tq
    qspec = pl.BlockSpec((GROUP, tq, HEAD_DIM), lambda b, k, i: (k, b * nq + i, 0))
    kvspec = pl.BlockSpec((1, seq, HEAD_DIM), lambda b, k, i: (k, b, 0))
    return pl.pallas_call(
        _win_kernel,
        grid=(batch, N_KV, nq),
        in_specs=[qspec, kvspec, kvspec],
        out_specs=qspec,
        out_shape=jax.ShapeDtypeStruct((N_HEADS, batch * seq, HEAD_DIM), F32),
        compiler_params=_params(("parallel", "parallel", "parallel")),
        name="winattn",
    )(qrot, kw, vw)


def _outproj_kernel(x_ref, y_ref, gn_ref, gate_ref, oc_ref, os_ref, ow_ref, w_ref, o_ref):
    tm = x_ref.shape[0]
    gates = jax.nn.sigmoid(gate_ref[...])
    heads = []
    for h in range(N_HEADS):
        g0 = gates[:, 3 * h:3 * h + 1]
        g1 = gates[:, 3 * h + 1:3 * h + 2]
        g2 = gates[:, 3 * h + 2:3 * h + 3]
        heads.append(g0 * oc_ref[h] + g1 * os_ref[h] + g2 * ow_ref[h])
    y_nsa = jnp.concatenate(heads, axis=-1) * jax.nn.silu(gn_ref[...])
    acc = jnp.dot(y_ref[...], w_ref[0:D_LRU, :], preferred_element_type=F32)
    acc = acc + jnp.dot(y_nsa.astype(BF16), w_ref[D_LRU:, :], preferred_element_type=F32)
    o_ref[...] = x_ref[...] + acc


def _outproj(x2, ylru, gn, gate, oc, osl, ow, w):
    m = x2.shape[0]
    tm = ROW_TILE
    row = lambda width: pl.BlockSpec((tm, width), lambda i: (i, 0))
    heads = pl.BlockSpec((N_HEADS, tm, HEAD_DIM), lambda i: (0, i, 0))
    return pl.pallas_call(
        _outproj_kernel,
        grid=(m // tm,),
        in_specs=[row(D_MODEL), row(D_LRU), row(D_NSA), row(LANE), heads, heads, heads,
                  pl.BlockSpec((D_MODEL, D_MODEL), lambda i: (0, 0))],
        out_specs=row(D_MODEL),
        out_shape=jax.ShapeDtypeStruct(x2.shape, F32),
        compiler_params=_params(("parallel",)),
        name="outproj",
    )(x2, ylru, gn, gate, oc, osl, ow, w)


def _norm_kernel(x_ref, g_ref, o_ref):
    x = x_ref[...]
    ms = jnp.mean(x * x, axis=-1, keepdims=True)
    o_ref[...] = (x * lax.rsqrt(ms + EPS)) * g_ref[...]


def _final_norm(x2, g):
    m = x2.shape[0]
    tm = ROW_TILE
    row = pl.BlockSpec((tm, D_MODEL), lambda i: (i, 0))
    return pl.pallas_call(
        _norm_kernel,
        grid=(m // tm,),
        in_specs=[row, pl.BlockSpec((1, D_MODEL), lambda i: (0, 0))],
        out_specs=row,
        out_shape=jax.ShapeDtypeStruct(x2.shape, F32),
        compiler_params=_params(("parallel",)),
        name="finalnorm",
    )(x2, g)


def _rope_tables(seq):
    inv = 1.0 / (ROPE_THETA ** (jnp.arange(0, HEAD_DIM, 2, dtype=F32) / HEAD_DIM))
    ang = jnp.arange(seq, dtype=F32)[:, None] * inv[None, :]
    cos, sin = jnp.cos(ang), jnp.sin(ang)
    reps = LANE // HEAD_DIM
    cosf = jnp.tile(jnp.concatenate([cos, cos], axis=-1), (1, reps))
    sinf = jnp.tile(jnp.concatenate([-sin, sin], axis=-1), (1, reps))
    return cosf, sinf


def _block_diag(w):
    h, n, _ = w.shape
    eye = jnp.eye(h, dtype=w.dtype)
    return (eye[:, None, :, None] * w[:, :, None, :]).reshape(h * n, h * n)


def _layer(x2, batch, seq, cosf, sinf, g, w_in, conv_w, conv_b, wa, ba, wi, bi, lam,
           pos_k, pos_v, kw1, kw2, vw1, vw2, w_out):
    w_in_p = jnp.pad(w_in, ((0, 0), (0, D_IN_PAD - D_IN))).astype(BF16)
    (u, gl, gn, gate, kc, vc, qraw, qrot, ks, vs, kwn, vwn) = _inproj(
        x2, g.reshape(1, D_MODEL), w_in_p, cosf, sinf, seq)

    wg = jnp.concatenate([_block_diag(wa), _block_diag(wi)], axis=1).astype(BF16)
    ylru = _lru(u, gl, conv_w, conv_b.reshape(1, D_LRU), wg, ba.reshape(1, D_LRU),
                bi.reshape(1, D_LRU), lam.reshape(1, D_LRU), batch, seq)

    ratio = CMP_LEN // CMP_STRIDE
    kcmp, vcmp = _compress(
        kc, vc, pos_k.reshape(ratio, SUB_FLAT), pos_v.reshape(ratio, SUB_FLAT),
        kw1.reshape(ratio, SUB_FLAT, CMP_HID).astype(BF16), kw2.astype(BF16),
        vw1.reshape(ratio, SUB_FLAT, CMP_HID).astype(BF16), vw2.astype(BF16), batch, seq)
    o_cmp, sel = _cmpsel(qraw, kcmp, vcmp, batch, seq)
    o_slc = _slc(qrot, ks, vs, sel, batch, seq)
    o_win = _win(qrot, kwn, vwn, batch, seq)
    return _outproj(x2, ylru, gn, gate, o_cmp, o_slc, o_win, w_out.astype(BF16))


def kernel(x, norm_g, w_in, conv_w, conv_b, lru_wa, lru_ba, lru_wi, lru_bi, lru_lambda, cmp_pos_k, cmp_pos_v, cmp_k_w1, cmp_k_w2, cmp_v_w1, cmp_v_w2, w_out, final_g):
    batch, seq, _ = x.shape
    assert seq % ROW_TILE == 0 and seq >= WINDOW + WIN_QBLK
    cosf, sinf = _rope_tables(seq)
    h = x.reshape(batch * seq, D_MODEL)
    for l in range(norm_g.shape[0]):
        h = _layer(h, batch, seq, cosf, sinf, norm_g[l], w_in[l], conv_w[l], conv_b[l],
                   lru_wa[l], lru_ba[l], lru_wi[l], lru_bi[l], lru_lambda[l],
                   cmp_pos_k[l], cmp_pos_v[l], cmp_k_w1[l], cmp_k_w2[l],
                   cmp_v_w1[l], cmp_v_w2[l], w_out[l])
    return _final_norm(h, final_g.reshape(1, D_MODEL)).reshape(batch, seq, D_MODEL)
```

```python
import functools

import numpy as np
import jax
import jax.numpy as jnp
from jax import lax
from jax.experimental import pallas as pl
from jax.experimental.pallas import tpu as pltpu

F32 = jnp.float32
BF16 = jnp.bfloat16

D_MODEL = 1024
D_LRU = 512
H_LRU = 8
LRU_BLK = D_LRU // H_LRU
CONV_W = 4
LRU_C = 8.0
D_NSA = 512
HEAD_DIM = 64
N_HEADS = 8
N_KV = 2
GROUP = N_HEADS // N_KV
KV_W = N_KV * HEAD_DIM
CMP_LEN = 32
CMP_STRIDE = 16
CMP_HID = 256
SLC_BLK = 64
SLC_TOPK = 16
WINDOW = 512
WIN_QBLK = 128
ROPE_THETA = 10000.0
EPS = 1e-6
NEG = -1e30
FORCE_BONUS = 1e4
SCALE = HEAD_DIM ** -0.5
D_IN = 2 * D_LRU + 2 * D_NSA + 6 * KV_W + 3 * N_HEADS
LANE = 128
D_IN_PAD = ((D_IN + LANE - 1) // LANE) * LANE
SUB_FLAT = CMP_STRIDE * HEAD_DIM

OFF_U = 0
OFF_GL = D_LRU
OFF_Q = 2 * D_LRU
OFF_GN = 2 * D_LRU + D_NSA
OFF_KV = 2 * D_LRU + 2 * D_NSA
OFF_GATE = OFF_KV + 6 * KV_W

ROW_TILE = 512
LRU_TILE = 256
CMP_QTILE = 256
ATT_QTILE = 256
SLC_CHUNK = 512
ATT_SUB = 128
V_ROWS = 80
LOG2E = 1.4426950408889634
VMEM_LIMIT = 48 * 1024 * 1024

_NT = (((1,), (1,)), ((), ()))


def _params(sem):
    return pltpu.CompilerParams(dimension_semantics=sem, vmem_limit_bytes=VMEM_LIMIT)


def _rope(t, cos, sin_signed):
    lane = lax.broadcasted_iota(jnp.int32, t.shape, 1)
    first = (lane % HEAD_DIM) < (HEAD_DIM // 2)
    rot = jnp.where(first, pltpu.roll(t, LANE - HEAD_DIM // 2, 1), pltpu.roll(t, HEAD_DIM // 2, 1))
    return t * cos + rot * sin_signed


def _inproj_kernel(x_ref, g_ref, w_ref, cos_ref, sin_ref,
                   u_ref, gl_ref, gn_ref, gate_ref, kc_ref, vc_ref,
                   qraw_ref, qt_ref, ks_ref, vst_ref, kw_ref, vwt_ref):
    tm = x_ref.shape[0]
    x = x_ref[...]
    ms = jnp.mean(x * x, axis=-1, keepdims=True)
    h = ((x * lax.rsqrt(ms + EPS)) * g_ref[...]).astype(BF16)
    cos = cos_ref[...]
    sin = sin_ref[...]

    def proj(off, width):
        return jnp.dot(h, w_ref[:, off:off + width], preferred_element_type=F32)

    u_ref[...] = proj(OFF_U, D_LRU)
    gl_ref[...] = proj(OFF_GL, D_LRU)
    gn_ref[...] = proj(OFF_GN, D_NSA)
    gate_ref[...] = proj(OFF_GATE, LANE)
    kc_ref[...] = proj(OFF_KV, KV_W)
    vc_ref[...] = proj(OFF_KV + KV_W, KV_W)

    def put_heads(ref, base, slab):
        ref[base] = slab[:, :HEAD_DIM].astype(ref.dtype)
        ref[base + 1] = slab[:, HEAD_DIM:].astype(ref.dtype)

    def put_heads_t(ref, base, slab_t):
        ref[base, 0:HEAD_DIM, :] = slab_t[:HEAD_DIM].astype(ref.dtype)
        ref[base + 1, 0:HEAD_DIM, :] = slab_t[HEAD_DIM:].astype(ref.dtype)

    for i in range(D_NSA // LANE):
        slab = proj(OFF_Q + i * LANE, LANE)
        put_heads(qraw_ref, 2 * i, slab)
        put_heads_t(qt_ref, 2 * i, (_rope(slab, cos, sin) * (SCALE * LOG2E)).T)

    ks = _rope(proj(OFF_KV + 2 * KV_W, KV_W), cos, sin)
    lane = lax.broadcasted_iota(jnp.int32, (tm, LANE), 1)
    blk = (lax.broadcasted_iota(jnp.int32, (tm, LANE), 0) // SLC_BLK) % (SLC_CHUNK // SLC_BLK)
    onehot = (lane - HEAD_DIM == blk).astype(F32)
    ks_ref[0] = jnp.where(lane < HEAD_DIM, ks, onehot).astype(ks_ref.dtype)
    ks_ref[1] = jnp.where(lane < HEAD_DIM, pltpu.roll(ks, HEAD_DIM, 1), onehot).astype(ks_ref.dtype)
    put_heads(kw_ref, 0, _rope(proj(OFF_KV + 4 * KV_W, KV_W), cos, sin))

    tail = (lax.broadcasted_iota(jnp.int32, (V_ROWS - HEAD_DIM, tm), 0) == 0).astype(BF16)
    for ref, off in ((vst_ref, OFF_KV + 3 * KV_W), (vwt_ref, OFF_KV + 5 * KV_W)):
        put_heads_t(ref, 0, proj(off, KV_W).T)
        for hh in range(N_KV):
            ref[hh, HEAD_DIM:V_ROWS, :] = tail


def _inproj(x2, g, w, cosf, sinf, seq):
    m = x2.shape[0]
    tm = ROW_TILE
    spt = seq // tm
    row = lambda width: pl.BlockSpec((tm, width), lambda i: (i, 0))
    assert tm % SLC_CHUNK == 0
    heads = lambda n, width: pl.BlockSpec((n, tm, width), lambda i: (0, i, 0))
    heads_t = lambda n, rows: pl.BlockSpec((n, rows, tm), lambda i: (0, 0, i))
    tab = pl.BlockSpec((tm, LANE), lambda i: (i % spt, 0))
    out_shape = (
        jax.ShapeDtypeStruct((m, D_LRU), F32),
        jax.ShapeDtypeStruct((m, D_LRU), F32),
        jax.ShapeDtypeStruct((m, D_NSA), F32),
        jax.ShapeDtypeStruct((m, LANE), F32),
        jax.ShapeDtypeStruct((m, KV_W), F32),
        jax.ShapeDtypeStruct((m, KV_W), F32),
        jax.ShapeDtypeStruct((N_HEADS, m, HEAD_DIM), BF16),
        jax.ShapeDtypeStruct((N_HEADS, HEAD_DIM, m), BF16),
        jax.ShapeDtypeStruct((N_KV, m, LANE), BF16),
        jax.ShapeDtypeStruct((N_KV, V_ROWS, m), BF16),
        jax.ShapeDtypeStruct((N_KV, m, HEAD_DIM), BF16),
        jax.ShapeDtypeStruct((N_KV, V_ROWS, m), BF16),
    )
    out_specs = (row(D_LRU), row(D_LRU), row(D_NSA), row(LANE), row(KV_W), row(KV_W),
                 heads(N_HEADS, HEAD_DIM), heads_t(N_HEADS, HEAD_DIM), heads(N_KV, LANE), heads_t(N_KV, V_ROWS),
                 heads(N_KV, HEAD_DIM), heads_t(N_KV, V_ROWS))
    return pl.pallas_call(
        _inproj_kernel,
        grid=(m // tm,),
        in_specs=[row(D_MODEL),
                  pl.BlockSpec((1, D_MODEL), lambda i: (0, 0)),
                  pl.BlockSpec((D_MODEL, D_IN_PAD), lambda i: (0, 0)),
                  tab, tab],
        out_specs=out_specs,
        out_shape=out_shape,
        compiler_params=_params(("parallel",)),
        name="inproj",
    )(x2, g, w, cosf, sinf)


def _softplus(x):
    return jnp.maximum(x, 0.0) + jnp.log1p(jnp.exp(-jnp.abs(x)))


def _lru_kernel(u_ref, g_ref, cw_ref, cb_ref, wg_ref, ba_ref, bi_ref, lam_ref, y_ref,
                ubuf, hprev, a0, b0, a1, b1):
    t = u_ref.shape[0]
    pad = t // 2
    head = 8

    @pl.when(pl.program_id(1) == 0)
    def _():
        ubuf[0:head, :] = jnp.zeros((head, D_LRU), F32)
        hprev[...] = jnp.zeros_like(hprev)

    ubuf[head:head + t, :] = u_ref[...]
    conv = cb_ref[...]
    for k in range(CONV_W):
        conv = conv + cw_ref[k:k + 1, :] * ubuf[pl.ds(head - (CONV_W - 1) + k, t), :]
    ubuf[head - (CONV_W - 1):head, :] = ubuf[head + t - (CONV_W - 1):head + t, :]

    pre = jnp.dot(conv.astype(BF16), wg_ref[...], preferred_element_type=F32)
    r = jax.nn.sigmoid(pre[:, :D_LRU] + ba_ref[...])
    gi = jax.nn.sigmoid(pre[:, D_LRU:] + bi_ref[...])
    log_a = (-LRU_C * r) * _softplus(-lam_ref[...])
    a = jnp.exp(log_a)
    drive = jnp.sqrt((1.0 - a) * (1.0 + a)) * (gi * conv)

    bufs = ((a0, b0), (a1, b1))
    for ab, bb in bufs:
        ab[0:pad, :] = jnp.ones((pad, D_LRU), F32)
        bb[0:pad, :] = jnp.zeros((pad, D_LRU), F32)
    a0[pad:pad + t, :] = a
    b0[pad:pad + t, :] = drive
    src = 0
    d = 1
    while d < t:
        sa, sb = bufs[src]
        da, db = bufs[1 - src]
        a_cur = sa[pad:pad + t, :]
        db[pad:pad + t, :] = sb[pad:pad + t, :] + a_cur * sb[pl.ds(pad - d, t), :]
        da[pad:pad + t, :] = a_cur * sa[pl.ds(pad - d, t), :]
        src = 1 - src
        d *= 2
    fa, fb = bufs[src]
    hseq = fa[pad:pad + t, :] * hprev[...] + fb[pad:pad + t, :]
    hprev[...] = hseq[t - 1:t, :]
    y_ref[...] = (hseq * jax.nn.silu(g_ref[...])).astype(y_ref.dtype)


def _lru(u, gl, cw, cb, wg, ba, bi, lam, batch, seq):
    t = LRU_TILE
    spt = seq // t
    row = pl.BlockSpec((t, D_LRU), lambda b, j: (b * spt + j, 0))
    vec = pl.BlockSpec((1, D_LRU), lambda b, j: (0, 0))
    pad = t // 2
    return pl.pallas_call(
        _lru_kernel,
        grid=(batch, spt),
        in_specs=[row, row,
                  pl.BlockSpec((CONV_W, D_LRU), lambda b, j: (0, 0)), vec,
                  pl.BlockSpec((D_LRU, 2 * D_LRU), lambda b, j: (0, 0)), vec, vec, vec],
        out_specs=row,
        out_shape=jax.ShapeDtypeStruct(u.shape, BF16),
        scratch_shapes=[pltpu.VMEM((8 + t, D_LRU), F32), pltpu.VMEM((1, D_LRU), F32)]
        + [pltpu.VMEM((pad + t, D_LRU), F32)] * 4,
        compiler_params=_params(("parallel", "arbitrary")),
        name="rglru",
    )(u, gl, cw, cb, wg, ba, bi, lam)


def _compress_kernel(k_ref, v_ref, pk_ref, pv_ref, k1_ref, k2_ref, v1_ref, v2_ref, ko_ref, vo_ref, shift):
    n = k_ref.shape[2]

    def one(x_ref, p_ref, w1_ref, w2_ref, o_ref):
        x = x_ref[0, 0]
        lo = jnp.dot((x + p_ref[0:1, :]).astype(BF16), w1_ref[0], preferred_element_type=F32)
        hi = jnp.dot((x + p_ref[1:2, :]).astype(BF16), w1_ref[1], preferred_element_type=F32)
        shift[0:n, :] = hi
        shift[n:n + 8, :] = jnp.zeros((8, CMP_HID), F32)
        pre = lo + shift[pl.ds(1, n), :]
        o = jnp.dot(jax.nn.silu(pre).astype(BF16), w2_ref[...], preferred_element_type=F32)
        o_ref[0, 0] = o.astype(o_ref.dtype)

    one(k_ref, pk_ref, k1_ref, k2_ref, ko_ref)
    one(v_ref, pv_ref, v1_ref, v2_ref, vo_ref)


def _compress(kc, vc, pk, pv, k1, k2, v1, v2, batch, seq):
    n = seq // CMP_STRIDE

    def relayout(t):
        t = t.reshape(batch, n, CMP_STRIDE, N_KV, HEAD_DIM)
        return jnp.transpose(t, (0, 3, 1, 2, 4)).reshape(batch, N_KV, n, SUB_FLAT)

    xin = pl.BlockSpec((1, 1, n, SUB_FLAT), lambda b, k: (b, k, 0, 0))
    pos = pl.BlockSpec((2, SUB_FLAT), lambda b, k: (0, 0))
    w1 = pl.BlockSpec((2, SUB_FLAT, CMP_HID), lambda b, k: (0, 0, 0))
    w2 = pl.BlockSpec((CMP_HID, HEAD_DIM), lambda b, k: (0, 0))
    out = pl.BlockSpec((1, 1, n, HEAD_DIM), lambda b, k: (b, k, 0, 0))
    oshape = jax.ShapeDtypeStruct((batch, N_KV, n, HEAD_DIM), BF16)
    return pl.pallas_call(
        _compress_kernel,
        grid=(batch, N_KV),
        in_specs=[xin, xin, pos, pos, w1, w2, w1, w2],
        out_specs=(out, out),
        out_shape=(oshape, oshape),
        scratch_shapes=[pltpu.VMEM((n + 8, CMP_HID), F32)],
        compiler_params=_params(("parallel", "parallel")),
        name="compress",
    )(relayout(kc), relayout(vc), pk, pv, k1, k2, v1, v2)


def _split3(x):
    hi = x.astype(BF16)
    r1 = x - hi.astype(F32)
    mid = r1.astype(BF16)
    lo = (r1 - mid.astype(F32)).astype(BF16)
    return hi, mid, lo


def _cmpsel_kernel(q_ref, kc_ref, vc_ref, mt_ref, o_ref, bias_ref, sc_ref):
    tq = q_ref.shape[1]
    ncp = kc_ref.shape[2]
    nb = mt_ref.shape[0]
    t0 = pl.program_id(2) * tq
    kc = kc_ref[0, 0]
    vc = vc_ref[0, 0]
    tok = t0 + lax.broadcasted_iota(jnp.int32, (tq, ncp), 0)
    blk_end = lax.broadcasted_iota(jnp.int32, (tq, ncp), 1) * CMP_STRIDE + (CMP_LEN - 1)
    valid = blk_end <= tok
    psum = jnp.zeros((tq, ncp), F32)
    for g in range(GROUP):
        s = lax.dot_general(q_ref[g], kc, _NT, preferred_element_type=F32) * SCALE
        s = jnp.where(valid, s, NEG)
        e = jnp.exp(s - jnp.max(s, axis=-1, keepdims=True))
        p = jnp.where(valid, e / jnp.sum(e, axis=-1, keepdims=True), 0.0)
        o_ref[g] = jnp.dot(p.astype(BF16), vc, preferred_element_type=F32)
        psum = psum + p

    mt = mt_ref[...]
    sc = jnp.zeros((nb, tq), F32)
    for part in _split3(psum):
        sc = sc + lax.dot_general(mt, part, _NT, preferred_element_type=F32)
    jj = lax.broadcasted_iota(jnp.int32, (nb, tq), 0)
    cur = (t0 + lax.broadcasted_iota(jnp.int32, (nb, tq), 1)) // SLC_BLK
    forced = (jj == 0) | (jj == cur) | (jj == cur - 1)
    sc = jnp.where(forced, sc + FORCE_BONUS, sc)
    sc = jnp.where(jj <= cur, sc, NEG)
    sc_ref[...] = sc

    def body(jp, cnt):
        r = sc_ref[pl.ds(jp, 1), :]
        before = (r > sc) | ((r == sc) & (jp < jj))
        return cnt + before.astype(jnp.int32)

    cnt = lax.fori_loop(0, nb, body, jnp.zeros((nb, tq), jnp.int32), unroll=8)
    selected = (cnt < min(SLC_TOPK, nb)) & (sc > NEG / 2)
    bias_ref[0, 0] = jnp.where(selected, 0.0, NEG)


def _overlap_t(seq):
    nc_pad = seq // CMP_STRIDE
    n_slc = seq // SLC_BLK
    c0 = np.arange(nc_pad)[None, :] * CMP_STRIDE
    j0 = np.arange(n_slc)[:, None] * SLC_BLK
    ov = np.clip(np.minimum(c0 + CMP_LEN, j0 + SLC_BLK) - np.maximum(c0, j0), 0, None) / CMP_LEN
    ov[:, nc_pad - 1] = 0.0
    return jnp.asarray(ov, dtype=BF16)


def _cmpsel(qraw, kcmp, vcmp, batch, seq):
    tq = CMP_QTILE
    spt = seq // tq
    ncp = seq // CMP_STRIDE
    nb = seq // SLC_BLK
    m = batch * seq
    qspec = pl.BlockSpec((GROUP, tq, HEAD_DIM), lambda b, k, i: (k, b * spt + i, 0))
    cspec = pl.BlockSpec((1, 1, ncp, HEAD_DIM), lambda b, k, i: (b, k, 0, 0))
    return pl.pallas_call(
        _cmpsel_kernel,
        grid=(batch, N_KV, spt),
        in_specs=[qspec, cspec, cspec, pl.BlockSpec((nb, ncp), lambda b, k, i: (0, 0))],
        out_specs=(qspec, pl.BlockSpec((1, 1, nb, tq), lambda b, k, i: (b, k, 0, i))),
        out_shape=(jax.ShapeDtypeStruct((N_HEADS, m, HEAD_DIM), F32),
                   jax.ShapeDtypeStruct((batch, N_KV, nb, seq), F32)),
        scratch_shapes=[pltpu.VMEM((nb, tq), F32)],
        compiler_params=_params(("parallel", "parallel", "parallel")),
        name="cmpsel",
    )(qraw, kcmp, vcmp, _overlap_t(seq))


def _attn_pieces(scores, vts, masks, carry):
    state = list(carry)
    for j, vt in enumerate(vts):
        for p, (m_i, acc) in enumerate(state):
            st = scores[j][p]
            if masks[j] is not None:
                st = jnp.where(masks[j], st, NEG)
            m_new = jnp.maximum(m_i, jnp.max(st, axis=0, keepdims=True))
            pt = jnp.exp2(st - m_new).astype(BF16)
            acc = jnp.exp2(m_i - m_new) * acc + jnp.dot(vt, pt, preferred_element_type=F32)
            state[p] = (m_new, acc)
    return tuple(state)


def _attn_init(tq):
    one = (jnp.full((1, 2 * tq), NEG, F32), jnp.zeros((V_ROWS, 2 * tq), F32))
    return (one,) * (GROUP // 2)


def _attn_queries(qt_ref):
    return [jnp.concatenate([qt_ref[2 * p], qt_ref[2 * p + 1]], axis=1) for p in range(GROUP // 2)]


def _attn_store(o_ref, carry):
    for p, (_, acc) in enumerate(carry):
        tq = acc.shape[1] // 2
        o = acc[:HEAD_DIM] / acc[HEAD_DIM:HEAD_DIM + 1]
        stacked = jnp.concatenate([o[:, :tq], o[:, tq:]], axis=0)
        o_ref[:, p * 2 * HEAD_DIM:(p + 1) * 2 * HEAD_DIM] = stacked.T


def _slc_kernel(qt_ref, k_ref, vt_ref, bias_ref, o_ref):
    tq = qt_ref.shape[2]
    ck = SLC_CHUNK
    sub = ATT_SUB
    bpc = ck // SLC_BLK
    i = pl.program_id(2)
    qs = _attn_queries(qt_ref)
    tok = i * tq + lax.broadcasted_iota(jnp.int32, (1, 2 * tq), 1) % tq
    zrows = jnp.zeros((LANE - HEAD_DIM - 2 * bpc, 2 * tq), BF16)

    def chunk(c, carry, causal):
        start = pl.multiple_of(c * ck, ck)
        b8 = bias_ref[0, 0, pl.ds(pl.multiple_of(c * bpc, bpc), bpc), :]
        b16 = jnp.concatenate([b8, jnp.zeros_like(b8)], axis=0).astype(BF16)
        brows = jnp.concatenate([b16, b16], axis=1)
        rhs = [jnp.concatenate([q, brows, zrows], axis=0) for q in qs]
        scores, vts, masks = [], [], []
        for j in range(ck // sub):
            kc = k_ref[0, pl.ds(start + j * sub, sub), :]
            scores.append([jnp.dot(kc, r, preferred_element_type=F32) for r in rhs])
            vts.append(vt_ref[0, :, pl.ds(start + j * sub, sub)])
            key = start + j * sub + lax.broadcasted_iota(jnp.int32, (sub, 2 * tq), 0)
            masks.append(key <= tok if causal else None)
        return _attn_pieces(scores, vts, masks, carry)

    n_chunks = (i * tq + tq + ck - 1) // ck
    carry = lax.fori_loop(0, n_chunks - 1, functools.partial(chunk, causal=False), _attn_init(tq))
    _attn_store(o_ref, chunk(n_chunks - 1, carry, causal=True))


def _slc(qt, ks, vst, bias, batch, seq):
    tq = ATT_QTILE
    nq = seq // tq
    nb = seq // SLC_BLK
    return pl.pallas_call(
        _slc_kernel,
        grid=(batch, N_KV, nq),
        in_specs=[pl.BlockSpec((GROUP, HEAD_DIM, tq), lambda b, k, i: (k, 0, b * nq + i)),
                  pl.BlockSpec((1, seq, LANE), lambda b, k, i: (k, b, 0)),
                  pl.BlockSpec((1, V_ROWS, seq), lambda b, k, i: (k, 0, b)),
                  pl.BlockSpec((1, 1, nb, tq), lambda b, k, i: (b, k, 0, i))],
        out_specs=pl.BlockSpec((tq, GROUP * HEAD_DIM), lambda b, k, i: (b * nq + i, k)),
        out_shape=jax.ShapeDtypeStruct((batch * seq, D_NSA), F32),
        compiler_params=_params(("parallel", "parallel", "parallel")),
        name="slcattn",
    )(qt, ks, vst, bias)


def _win_kernel(qt_ref, k_ref, vt_ref, o_ref):
    tq = qt_ref.shape[2]
    sub = ATT_SUB
    n = pl.program_id(2)
    qs = _attn_queries(qt_ref)
    tok = n * tq + lax.broadcasted_iota(jnp.int32, (1, 2 * tq), 1) % tq
    scores, vts, masks = [], [], []
    for back in range(WINDOW // tq + 1):
        kb = n - back
        start = pl.multiple_of(jnp.maximum(kb, 0) * tq, tq)
        for j in range(tq // sub):
            kc = k_ref[0, pl.ds(start + j * sub, sub), :]
            scores.append([jnp.dot(kc, q, preferred_element_type=F32) for q in qs])
            vts.append(vt_ref[0, :, pl.ds(start + j * sub, sub)])
            diff = tok - (start + j * sub + lax.broadcasted_iota(jnp.int32, (sub, 2 * tq), 0))
            masks.append((diff >= 0) & (diff < WINDOW) & (kb >= 0))
    _attn_store(o_ref, _attn_pieces(scores, vts, masks, _attn_init(tq)))


def _win(qt, kw, vwt, batch, seq):
    tq = ATT_QTILE
    nq = seq // tq
    return pl.pallas_call(
        _win_kernel,
        grid=(batch, N_KV, nq),
        in_specs=[pl.BlockSpec((GROUP, HEAD_DIM, tq), lambda b, k, i: (k, 0, b * nq + i)),
                  pl.BlockSpec((1, seq, HEAD_DIM), lambda b, k, i: (k, b, 0)),
                  pl.BlockSpec((1, V_ROWS, seq), lambda b, k, i: (k, 0, b))],
        out_specs=pl.BlockSpec((tq, GROUP * HEAD_DIM), lambda b, k, i: (b * nq + i, k)),
        out_shape=jax.ShapeDtypeStruct((batch * seq, D_NSA), F32),
        compiler_params=_params(("parallel", "parallel", "parallel")),
        name="winattn",
    )(qt, kw, vwt)


def _outproj_kernel(x_ref, y_ref, gn_ref, gate_ref, oc_ref, os_ref, ow_ref, w_ref, o_ref):
    tm = x_ref.shape[0]
    gates = jax.nn.sigmoid(gate_ref[...])

    def spread(branch):
        return jnp.concatenate(
            [jnp.broadcast_to(gates[:, 3 * h + branch:3 * h + branch + 1], (tm, HEAD_DIM)) for h in range(N_HEADS)],
            axis=-1)

    o_cmp = jnp.concatenate([oc_ref[h] for h in range(N_HEADS)], axis=-1)
    y_nsa = (spread(0) * o_cmp + spread(1) * os_ref[...] + spread(2) * ow_ref[...]) * jax.nn.silu(gn_ref[...])
    acc = jnp.dot(y_ref[...], w_ref[0:D_LRU, :], preferred_element_type=F32)
    acc = acc + jnp.dot(y_nsa.astype(BF16), w_ref[D_LRU:, :], preferred_element_type=F32)
    o_ref[...] = x_ref[...] + acc


def _outproj(x2, ylru, gn, gate, oc, osl, ow, w):
    m = x2.shape[0]
    tm = ROW_TILE
    row = lambda width: pl.BlockSpec((tm, width), lambda i: (i, 0))
    heads = pl.BlockSpec((N_HEADS, tm, HEAD_DIM), lambda i: (0, i, 0))
    return pl.pallas_call(
        _outproj_kernel,
        grid=(m // tm,),
        in_specs=[row(D_MODEL), row(D_LRU), row(D_NSA), row(LANE), heads, row(D_NSA), row(D_NSA),
                  pl.BlockSpec((D_MODEL, D_MODEL), lambda i: (0, 0))],
        out_specs=row(D_MODEL),
        out_shape=jax.ShapeDtypeStruct(x2.shape, F32),
        compiler_params=_params(("parallel",)),
        name="outproj",
    )(x2, ylru, gn, gate, oc, osl, ow, w)


def _norm_kernel(x_ref, g_ref, o_ref):
    x = x_ref[...]
    ms = jnp.mean(x * x, axis=-1, keepdims=True)
    o_ref[...] = (x * lax.rsqrt(ms + EPS)) * g_ref[...]


def _final_norm(x2, g):
    m = x2.shape[0]
    tm = ROW_TILE
    row = pl.BlockSpec((tm, D_MODEL), lambda i: (i, 0))
    return pl.pallas_call(
        _norm_kernel,
        grid=(m // tm,),
        in_specs=[row, pl.BlockSpec((1, D_MODEL), lambda i: (0, 0))],
        out_specs=row,
        out_shape=jax.ShapeDtypeStruct(x2.shape, F32),
        compiler_params=_params(("parallel",)),
        name="finalnorm",
    )(x2, g)


def _rope_tables(seq):
    inv = 1.0 / (ROPE_THETA ** (jnp.arange(0, HEAD_DIM, 2, dtype=F32) / HEAD_DIM))
    ang = jnp.arange(seq, dtype=F32)[:, None] * inv[None, :]
    cos, sin = jnp.cos(ang), jnp.sin(ang)
    reps = LANE // HEAD_DIM
    cosf = jnp.tile(jnp.concatenate([cos, cos], axis=-1), (1, reps))
    sinf = jnp.tile(jnp.concatenate([-sin, sin], axis=-1), (1, reps))
    return cosf, sinf


def _block_diag(w):
    h, n, _ = w.shape
    eye = jnp.eye(h, dtype=w.dtype)
    return (eye[:, None, :, None] * w[:, :, None, :]).reshape(h * n, h * n)


def _layer(x2, batch, seq, cosf, sinf, g, w_in, conv_w, conv_b, wa, ba, wi, bi, lam,
           pos_k, pos_v, kw1, kw2, vw1, vw2, w_out):
    w_in_p = jnp.pad(w_in, ((0, 0), (0, D_IN_PAD - D_IN))).astype(BF16)
    (u, gl, gn, gate, kc, vc, qraw, qt, ks, vst, kwn, vwt) = _inproj(
        x2, g.reshape(1, D_MODEL), w_in_p, cosf, sinf, seq)

    wg = jnp.concatenate([_block_diag(wa), _block_diag(wi)], axis=1).astype(BF16)
    ylru = _lru(u, gl, conv_w, conv_b.reshape(1, D_LRU), wg, ba.reshape(1, D_LRU),
                bi.reshape(1, D_LRU), lam.reshape(1, D_LRU), batch, seq)

    ratio = CMP_LEN // CMP_STRIDE
    kcmp, vcmp = _compress(
        kc, vc, pos_k.reshape(ratio, SUB_FLAT), pos_v.reshape(ratio, SUB_FLAT),
        kw1.reshape(ratio, SUB_FLAT, CMP_HID).astype(BF16), kw2.astype(BF16),
        vw1.reshape(ratio, SUB_FLAT, CMP_HID).astype(BF16), vw2.astype(BF16), batch, seq)
    o_cmp, bias = _cmpsel(qraw, kcmp, vcmp, batch, seq)
    o_slc = _slc(qt, ks, vst, bias, batch, seq)
    o_win = _win(qt, kwn, vwt, batch, seq)
    return _outproj(x2, ylru, gn, gate, o_cmp, o_slc, o_win, w_out.astype(BF16))


def kernel(x, norm_g, w_in, conv_w, conv_b, lru_wa, lru_ba, lru_wi, lru_bi, lru_lambda, cmp_pos_k, cmp_pos_v, cmp_k_w1, cmp_k_w2, cmp_v_w1, cmp_v_w2, w_out, final_g):
    batch, seq, _ = x.shape
    assert seq % ROW_TILE == 0 and seq % SLC_CHUNK == 0
    cosf, sinf = _rope_tables(seq)
    h = x.reshape(batch * seq, D_MODEL)
    for l in range(norm_g.shape[0]):
        h = _layer(h, batch, seq, cosf, sinf, norm_g[l], w_in[l], conv_w[l], conv_b[l],
                   lru_wa[l], lru_ba[l], lru_wi[l], lru_bi[l], lru_lambda[l],
                   cmp_pos_k[l], cmp_pos_v[l], cmp_k_w1[l], cmp_k_w2[l],
                   cmp_v_w1[l], cmp_v_w2[l], w_out[l])
    return _final_norm(h, final_g.reshape(1, D_MODEL)).reshape(batch, seq, D_MODEL)
```

```python
import functools

import numpy as np
import jax
import jax.numpy as jnp
from jax import lax
from jax.experimental import pallas as pl
from jax.experimental.pallas import tpu as pltpu

F32 = jnp.float32
BF16 = jnp.bfloat16

D_MODEL = 1024
D_LRU = 512
H_LRU = 8
LRU_BLK = D_LRU // H_LRU
CONV_W = 4
LRU_C = 8.0
D_NSA = 512
HEAD_DIM = 64
N_HEADS = 8
N_KV = 2
GROUP = N_HEADS // N_KV
KV_W = N_KV * HEAD_DIM
CMP_LEN = 32
CMP_STRIDE = 16
CMP_HID = 256
SLC_BLK = 64
SLC_TOPK = 16
WINDOW = 512
ROPE_THETA = 10000.0
EPS = 1e-6
NEG = -1e30
FORCE_BONUS = 1e4
SCALE = HEAD_DIM ** -0.5
LOG2E = 1.4426950408889634
D_IN = 2 * D_LRU + 2 * D_NSA + 6 * KV_W + 3 * N_HEADS
LANE = 128
SUBLANE = 8
D_IN_PAD = ((D_IN + LANE - 1) // LANE) * LANE
SUB_FLAT = CMP_STRIDE * HEAD_DIM
N_BRANCH = 3

OFF_U = 0
OFF_GL = D_LRU
OFF_Q = 2 * D_LRU
OFF_GN = 2 * D_LRU + D_NSA
OFF_KV = 2 * D_LRU + 2 * D_NSA

ROW_TILE = 512
LRU_TILE = 256
ATT_QTILE = 256
SLC_CHUNK = 512
ATT_SUB = 128
V_ROWS = 80
VMEM_LIMIT = 48 * 1024 * 1024

_NT = (((1,), (1,)), ((), ()))


def _params(sem):
    return pltpu.CompilerParams(dimension_semantics=sem, vmem_limit_bytes=VMEM_LIMIT)


def _rope(t, cos, sin_signed):
    lane = lax.broadcasted_iota(jnp.int32, t.shape, 1)
    first = (lane % HEAD_DIM) < (HEAD_DIM // 2)
    rot = jnp.where(first, pltpu.roll(t, LANE - HEAD_DIM // 2, 1), pltpu.roll(t, HEAD_DIM // 2, 1))
    return t * cos + rot * sin_signed


def _inproj_kernel(x_ref, g_ref, w_ref, cos_ref, sin_ref,
                   u_ref, gl_ref, gn_ref, gt_ref, kcr_ref, vcr_ref,
                   qct_ref, qt_ref, ks_ref, vst_ref, kw_ref, vwt_ref, kc_scr, vc_scr):
    tm = x_ref.shape[0]
    x = x_ref[...]
    ms = jnp.mean(x * x, axis=-1, keepdims=True)
    h = ((x * lax.rsqrt(ms + EPS)) * g_ref[...]).astype(BF16)
    cos = cos_ref[...]
    sin = sin_ref[...]

    def proj(off, width):
        return jnp.dot(h, w_ref[:, off:off + width], preferred_element_type=F32)

    u_ref[...] = proj(OFF_U, D_LRU)
    gl_ref[...] = proj(OFF_GL, D_LRU)
    gn_ref[...] = proj(OFF_GN, D_NSA)
    q = proj(OFF_Q, D_NSA)
    kvg = proj(OFF_KV, D_IN_PAD - OFF_KV)
    group = lambda i: kvg[:, i * KV_W:(i + 1) * KV_W]

    def put_heads_t(ref, base, slab_t):
        ref[base, 0:HEAD_DIM, :] = slab_t[:HEAD_DIM].astype(ref.dtype)
        ref[base + 1, 0:HEAD_DIM, :] = slab_t[HEAD_DIM:].astype(ref.dtype)

    for i in range(D_NSA // LANE):
        slab = q[:, i * LANE:(i + 1) * LANE]
        put_heads_t(qct_ref, 2 * i, (slab * (SCALE * LOG2E)).T)
        put_heads_t(qt_ref, 2 * i, (_rope(slab, cos, sin) * (SCALE * LOG2E)).T)

    for scr, dst, i in ((kc_scr, kcr_ref, 0), (vc_scr, vcr_ref, 1)):
        scr[...] = group(i)
        for j in range(CMP_STRIDE):
            rows = scr[pl.ds(j, tm // CMP_STRIDE, stride=CMP_STRIDE), :]
            for hh in range(N_KV):
                dst[hh, :, j * HEAD_DIM:(j + 1) * HEAD_DIM] = rows[:, hh * HEAD_DIM:(hh + 1) * HEAD_DIM]

    ks = _rope(group(2), cos, sin)
    lane = lax.broadcasted_iota(jnp.int32, (tm, LANE), 1)
    blk = (lax.broadcasted_iota(jnp.int32, (tm, LANE), 0) // SLC_BLK) % (SLC_CHUNK // SLC_BLK)
    onehot = (lane - HEAD_DIM == blk).astype(F32)
    ks_ref[0] = jnp.where(lane < HEAD_DIM, ks, onehot).astype(ks_ref.dtype)
    ks_ref[1] = jnp.where(lane < HEAD_DIM, pltpu.roll(ks, HEAD_DIM, 1), onehot).astype(ks_ref.dtype)
    kw = _rope(group(4), cos, sin)
    kw_ref[0] = kw[:, :HEAD_DIM].astype(kw_ref.dtype)
    kw_ref[1] = kw[:, HEAD_DIM:].astype(kw_ref.dtype)

    tail = (lax.broadcasted_iota(jnp.int32, (V_ROWS - HEAD_DIM, tm), 0) == 0).astype(BF16)
    for ref, i in ((vst_ref, 3), (vwt_ref, 5)):
        put_heads_t(ref, 0, group(i).T)
        for hh in range(N_KV):
            ref[hh, HEAD_DIM:V_ROWS, :] = tail

    gate_t = group(6).T
    fill = jnp.zeros((SUBLANE - N_BRANCH, tm), F32)
    for hd in range(N_HEADS):
        gt_ref[hd] = jnp.concatenate([gate_t[N_BRANCH * hd:N_BRANCH * (hd + 1)], fill], axis=0)


def _inproj(x2, g, w, cosf, sinf, seq):
    m = x2.shape[0]
    tm = ROW_TILE
    spt = seq // tm
    assert tm % SLC_CHUNK == 0
    row = lambda width: pl.BlockSpec((tm, width), lambda i: (i, 0))
    heads = lambda n, width: pl.BlockSpec((n, tm, width), lambda i: (0, i, 0))
    heads_t = lambda n, rows: pl.BlockSpec((n, rows, tm), lambda i: (0, 0, i))
    flat = pl.BlockSpec((N_KV, tm // CMP_STRIDE, SUB_FLAT), lambda i: (0, i, 0))
    tab = pl.BlockSpec((tm, LANE), lambda i: (i % spt, 0))
    out_shape = (
        jax.ShapeDtypeStruct((m, D_LRU), F32),
        jax.ShapeDtypeStruct((m, D_LRU), F32),
        jax.ShapeDtypeStruct((m, D_NSA), F32),
        jax.ShapeDtypeStruct((N_HEADS, SUBLANE, m), F32),
        jax.ShapeDtypeStruct((N_KV, m // CMP_STRIDE, SUB_FLAT), F32),
        jax.ShapeDtypeStruct((N_KV, m // CMP_STRIDE, SUB_FLAT), F32),
        jax.ShapeDtypeStruct((N_HEADS, HEAD_DIM, m), BF16),
        jax.ShapeDtypeStruct((N_HEADS, HEAD_DIM, m), BF16),
        jax.ShapeDtypeStruct((N_KV, m, LANE), BF16),
        jax.ShapeDtypeStruct((N_KV, V_ROWS, m), BF16),
        jax.ShapeDtypeStruct((N_KV, m, HEAD_DIM), BF16),
        jax.ShapeDtypeStruct((N_KV, V_ROWS, m), BF16),
    )
    out_specs = (row(D_LRU), row(D_LRU), row(D_NSA), heads_t(N_HEADS, SUBLANE), flat, flat,
                 heads_t(N_HEADS, HEAD_DIM), heads_t(N_HEADS, HEAD_DIM), heads(N_KV, LANE), heads_t(N_KV, V_ROWS),
                 heads(N_KV, HEAD_DIM), heads_t(N_KV, V_ROWS))
    return pl.pallas_call(
        _inproj_kernel,
        grid=(m // tm,),
        in_specs=[row(D_MODEL),
                  pl.BlockSpec((1, D_MODEL), lambda i: (0, 0)),
                  pl.BlockSpec((D_MODEL, D_IN_PAD), lambda i: (0, 0)),
                  tab, tab],
        out_specs=out_specs,
        out_shape=out_shape,
        scratch_shapes=[pltpu.VMEM((tm, KV_W), F32)] * 2,
        compiler_params=_params(("parallel",)),
        name="inproj",
    )(x2, g, w, cosf, sinf)


def _softplus(x):
    return jnp.maximum(x, 0.0) + jnp.log1p(jnp.exp(-jnp.abs(x)))


def _lru_kernel(u_ref, g_ref, cw_ref, cb_ref, wg_ref, ba_ref, bi_ref, lam_ref, y_ref,
                ubuf, hprev, hseq):
    t = u_ref.shape[0]
    head = SUBLANE

    @pl.when(pl.program_id(1) == 0)
    def _():
        ubuf[0:head, :] = jnp.zeros((head, D_LRU), F32)
        hprev[...] = jnp.zeros_like(hprev)

    ubuf[head:head + t, :] = u_ref[...]
    conv = cb_ref[...]
    for k in range(CONV_W):
        conv = conv + cw_ref[k:k + 1, :] * ubuf[pl.ds(head - (CONV_W - 1) + k, t), :]
    ubuf[head - (CONV_W - 1):head, :] = ubuf[head + t - (CONV_W - 1):head + t, :]

    pre = jnp.dot(conv.astype(BF16), wg_ref[...], preferred_element_type=F32)
    r = jax.nn.sigmoid(pre[:, :D_LRU] + ba_ref[...])
    gi = jax.nn.sigmoid(pre[:, D_LRU:] + bi_ref[...])
    log_a = (-LRU_C * r) * _softplus(-lam_ref[...])
    a = jnp.exp(log_a)
    drive = jnp.sqrt((1.0 - a) * (1.0 + a)) * (gi * conv)

    ng = t // SUBLANE
    a3 = a.reshape(ng, SUBLANE, D_LRU)
    b3 = drive.reshape(ng, SUBLANE, D_LRU)
    sub = lax.broadcasted_iota(jnp.int32, (ng, SUBLANE, D_LRU), 1)
    d = 1
    while d < SUBLANE:
        keep = sub >= d
        b3 = jnp.where(keep, b3 + a3 * pltpu.roll(b3, d, 1), b3)
        a3 = jnp.where(keep, a3 * pltpu.roll(a3, d, 1), a3)
        d *= 2
    h_in = jnp.broadcast_to(hprev[...], (SUBLANE, D_LRU))
    for gidx in range(ng):
        h_g = a3[gidx] * h_in + b3[gidx]
        hseq[gidx * SUBLANE:(gidx + 1) * SUBLANE, :] = h_g
        h_in = jnp.broadcast_to(h_g[SUBLANE - 1:SUBLANE, :], (SUBLANE, D_LRU))
    hprev[...] = h_in[0:1, :]
    y_ref[...] = (hseq[...] * jax.nn.silu(g_ref[...])).astype(y_ref.dtype)


def _lru(u, gl, cw, cb, wg, ba, bi, lam, batch, seq):
    t = LRU_TILE
    spt = seq // t
    row = pl.BlockSpec((t, D_LRU), lambda b, j: (b * spt + j, 0))
    vec = pl.BlockSpec((1, D_LRU), lambda b, j: (0, 0))
    return pl.pallas_call(
        _lru_kernel,
        grid=(batch, spt),
        in_specs=[row, row,
                  pl.BlockSpec((CONV_W, D_LRU), lambda b, j: (0, 0)), vec,
                  pl.BlockSpec((D_LRU, 2 * D_LRU), lambda b, j: (0, 0)), vec, vec, vec],
        out_specs=row,
        out_shape=jax.ShapeDtypeStruct(u.shape, BF16),
        scratch_shapes=[pltpu.VMEM((SUBLANE + t, D_LRU), F32), pltpu.VMEM((1, D_LRU), F32),
                        pltpu.VMEM((t, D_LRU), F32)],
        compiler_params=_params(("parallel", "arbitrary")),
        name="rglru",
    )(u, gl, cw, cb, wg, ba, bi, lam)


def _compress_kernel(k_ref, v_ref, pk_ref, pv_ref, k1_ref, k2_ref, v1_ref, v2t_ref, ko_ref, vo_ref, shift):
    n = k_ref.shape[1]

    def hidden(x_ref, p_ref, w1_ref):
        x = x_ref[0]
        lo = jnp.dot((x + p_ref[0:1, :]).astype(BF16), w1_ref[0], preferred_element_type=F32)
        hi = jnp.dot((x + p_ref[1:2, :]).astype(BF16), w1_ref[1], preferred_element_type=F32)
        shift[0:n, :] = hi
        shift[n:n + 8, :] = jnp.zeros((8, CMP_HID), F32)
        return jax.nn.silu(lo + shift[pl.ds(1, n), :]).astype(BF16)

    ko_ref[0, 0] = jnp.dot(hidden(k_ref, pk_ref, k1_ref), k2_ref[...],
                           preferred_element_type=F32).astype(ko_ref.dtype)
    vo_ref[0, 0] = lax.dot_general(v2t_ref[...], hidden(v_ref, pv_ref, v1_ref), _NT,
                                   preferred_element_type=F32).astype(vo_ref.dtype)


def _compress(kcr, vcr, pk, pv, k1, k2, v1, v2t, batch, seq):
    n = seq // CMP_STRIDE
    xin = pl.BlockSpec((1, n, SUB_FLAT), lambda b, k: (k, b, 0))
    pos = pl.BlockSpec((2, SUB_FLAT), lambda b, k: (0, 0))
    w1 = pl.BlockSpec((2, SUB_FLAT, CMP_HID), lambda b, k: (0, 0, 0))
    return pl.pallas_call(
        _compress_kernel,
        grid=(batch, N_KV),
        in_specs=[xin, xin, pos, pos, w1, pl.BlockSpec((CMP_HID, HEAD_DIM), lambda b, k: (0, 0)),
                  w1, pl.BlockSpec((HEAD_DIM, CMP_HID), lambda b, k: (0, 0))],
        out_specs=(pl.BlockSpec((1, 1, n, HEAD_DIM), lambda b, k: (b, k, 0, 0)),
                   pl.BlockSpec((1, 1, HEAD_DIM, n), lambda b, k: (b, k, 0, 0))),
        out_shape=(jax.ShapeDtypeStruct((batch, N_KV, n, HEAD_DIM), BF16),
                   jax.ShapeDtypeStruct((batch, N_KV, HEAD_DIM, n), BF16)),
        scratch_shapes=[pltpu.VMEM((n + 8, CMP_HID), F32)],
        compiler_params=_params(("parallel", "parallel")),
        name="compress",
    )(kcr, vcr, pk, pv, k1, k2, v1, v2t)


def _attn_queries(qt_ref):
    return [jnp.concatenate([qt_ref[2 * p], qt_ref[2 * p + 1]], axis=1) for p in range(GROUP // 2)]


def _attn_store(o_ref, gt_ref, branch, outs):
    for p, o in enumerate(outs):
        tq = o.shape[1] // 2
        ga = jax.nn.sigmoid(gt_ref[2 * p, branch:branch + 1, :])
        gb = jax.nn.sigmoid(gt_ref[2 * p + 1, branch:branch + 1, :])
        stacked = jnp.concatenate([ga * o[:, :tq], gb * o[:, tq:]], axis=0)
        o_ref[:, p * 2 * HEAD_DIM:(p + 1) * 2 * HEAD_DIM] = stacked.T


def _attn_pieces(scores, vts, masks, carry):
    state = list(carry)
    for j, vt in enumerate(vts):
        for p, (m_i, acc) in enumerate(state):
            st = scores[j][p]
            if masks[j] is not None:
                st = jnp.where(masks[j], st, NEG)
            m_new = jnp.maximum(m_i, jnp.max(st, axis=0, keepdims=True))
            pt = jnp.exp2(st - m_new).astype(BF16)
            acc = jnp.exp2(m_i - m_new) * acc + jnp.dot(vt, pt, preferred_element_type=F32)
            state[p] = (m_new, acc)
    return tuple(state)


def _attn_init(tq):
    one = (jnp.full((1, 2 * tq), NEG, F32), jnp.zeros((V_ROWS, 2 * tq), F32))
    return (one,) * (GROUP // 2)


def _attn_finish(carry):
    return [acc[:HEAD_DIM] / acc[HEAD_DIM:HEAD_DIM + 1] for _, acc in carry]


def _split3(x):
    hi = x.astype(BF16)
    r1 = x - hi.astype(F32)
    mid = r1.astype(BF16)
    lo = (r1 - mid.astype(F32)).astype(BF16)
    return hi, mid, lo


def _cmpsel_kernel(qt_ref, gt_ref, kc_ref, vct_ref, mt_ref, o_ref, bias_ref):
    tq = qt_ref.shape[2]
    ncp = kc_ref.shape[2]
    nb = mt_ref.shape[0]
    t0 = pl.program_id(2) * tq
    kc = kc_ref[0, 0]
    vct = vct_ref[0, 0]
    tok = t0 + lax.broadcasted_iota(jnp.int32, (1, 2 * tq), 1) % tq
    blk_end = lax.broadcasted_iota(jnp.int32, (ncp, 2 * tq), 0) * CMP_STRIDE + (CMP_LEN - 1)
    valid = blk_end <= tok
    any_valid = tok >= CMP_LEN - 1
    psum = jnp.zeros((ncp, tq), F32)
    outs = []
    for q in _attn_queries(qt_ref):
        st = jnp.where(valid, jnp.dot(kc, q, preferred_element_type=F32), NEG)
        e = jnp.exp2(st - jnp.max(st, axis=0, keepdims=True))
        inv = jnp.where(any_valid, 1.0 / jnp.sum(e, axis=0, keepdims=True), 0.0)
        p = e * inv
        outs.append(jnp.dot(vct, p.astype(BF16), preferred_element_type=F32))
        psum = psum + (p[:, :tq] + p[:, tq:])
    _attn_store(o_ref, gt_ref, 0, outs)

    mt = mt_ref[...]
    sc = jnp.zeros((nb, tq), F32)
    for part in _split3(psum):
        sc = sc + jnp.dot(mt, part, preferred_element_type=F32)
    jj = lax.broadcasted_iota(jnp.int32, (nb, tq), 0)
    cur = (t0 + lax.broadcasted_iota(jnp.int32, (nb, tq), 1)) // SLC_BLK
    forced = (jj == 0) | (jj == cur) | (jj == cur - 1)
    sc = jnp.where(forced, sc + FORCE_BONUS, sc)
    sc = jnp.where(jj <= cur, sc, NEG)

    ngrp = nb // SUBLANE
    rows = [sc[g * SUBLANE:(g + 1) * SUBLANE] for g in range(ngrp)]
    cnt = [jnp.zeros((SUBLANE, tq), jnp.int32) for _ in range(ngrp)]
    sub = lax.broadcasted_iota(jnp.int32, (SUBLANE, tq), 0)
    for gp in range(ngrp):
        for r in range(SUBLANE):
            other = jnp.broadcast_to(rows[gp][r:r + 1], (SUBLANE, tq))
            for g in range(ngrp):
                if g < gp:
                    before = other > rows[g]
                elif g > gp:
                    before = other >= rows[g]
                else:
                    before = (other > rows[g]) | ((other == rows[g]) & (sub > r))
                cnt[g] = cnt[g] + before.astype(jnp.int32)
    for g in range(ngrp):
        selected = (cnt[g] < min(SLC_TOPK, nb)) & (rows[g] > NEG / 2)
        bias_ref[0, 0, g * SUBLANE:(g + 1) * SUBLANE, :] = jnp.where(selected, 0.0, NEG)


def _overlap_t(seq):
    nc_pad = seq // CMP_STRIDE
    n_slc = seq // SLC_BLK
    c0 = np.arange(nc_pad)[None, :] * CMP_STRIDE
    j0 = np.arange(n_slc)[:, None] * SLC_BLK
    ov = np.clip(np.minimum(c0 + CMP_LEN, j0 + SLC_BLK) - np.maximum(c0, j0), 0, None) / CMP_LEN
    ov[:, nc_pad - 1] = 0.0
    return jnp.asarray(ov, dtype=BF16)


def _cmpsel(qct, gt, kcmp, vcmpt, batch, seq):
    tq = ATT_QTILE
    nq = seq // tq
    ncp = seq // CMP_STRIDE
    nb = seq // SLC_BLK
    return pl.pallas_call(
        _cmpsel_kernel,
        grid=(batch, N_KV, nq),
        in_specs=[pl.BlockSpec((GROUP, HEAD_DIM, tq), lambda b, k, i: (k, 0, b * nq + i)),
                  pl.BlockSpec((GROUP, SUBLANE, tq), lambda b, k, i: (k, 0, b * nq + i)),
                  pl.BlockSpec((1, 1, ncp, HEAD_DIM), lambda b, k, i: (b, k, 0, 0)),
                  pl.BlockSpec((1, 1, HEAD_DIM, ncp), lambda b, k, i: (b, k, 0, 0)),
                  pl.BlockSpec((nb, ncp), lambda b, k, i: (0, 0))],
        out_specs=(pl.BlockSpec((tq, GROUP * HEAD_DIM), lambda b, k, i: (b * nq + i, k)),
                   pl.BlockSpec((1, 1, nb, tq), lambda b, k, i: (b, k, 0, i))),
        out_shape=(jax.ShapeDtypeStruct((batch * seq, D_NSA), F32),
                   jax.ShapeDtypeStruct((batch, N_KV, nb, seq), F32)),
        compiler_params=_params(("parallel", "parallel", "parallel")),
        name="cmpsel",
    )(qct, gt, kcmp, vcmpt, _overlap_t(seq))


def _slc_kernel(qt_ref, gt_ref, k_ref, vt_ref, bias_ref, o_ref):
    tq = qt_ref.shape[2]
    ck = SLC_CHUNK
    sub = ATT_SUB
    bpc = ck // SLC_BLK
    i = pl.program_id(2)
    qs = _attn_queries(qt_ref)
    tok = i * tq + lax.broadcasted_iota(jnp.int32, (1, 2 * tq), 1) % tq
    zrows = jnp.zeros((LANE - HEAD_DIM - 2 * bpc, 2 * tq), BF16)

    def chunk(c, carry, causal):
        start = pl.multiple_of(c * ck, ck)
        b8 = bias_ref[0, 0, pl.ds(pl.multiple_of(c * bpc, bpc), bpc), :]
        b16 = jnp.concatenate([b8, jnp.zeros_like(b8)], axis=0).astype(BF16)
        brows = jnp.concatenate([b16, b16], axis=1)
        rhs = [jnp.concatenate([q, brows, zrows], axis=0) for q in qs]
        scores, vts, masks = [], [], []
        for j in range(ck // sub):
            kc = k_ref[0, pl.ds(start + j * sub, sub), :]
            scores.append([jnp.dot(kc, r, preferred_element_type=F32) for r in rhs])
            vts.append(vt_ref[0, :, pl.ds(start + j * sub, sub)])
            key = start + j * sub + lax.broadcasted_iota(jnp.int32, (sub, 2 * tq), 0)
            masks.append(key <= tok if causal else None)
        return _attn_pieces(scores, vts, masks, carry)

    n_chunks = (i * tq + tq + ck - 1) // ck
    carry = lax.fori_loop(0, n_chunks - 1, functools.partial(chunk, causal=False), _attn_init(tq))
    carry = chunk(n_chunks - 1, carry, causal=True)
    _attn_store(o_ref, gt_ref, 1, _attn_finish(carry))


def _slc(qt, gt, ks, vst, bias, batch, seq):
    tq = ATT_QTILE
    nq = seq // tq
    nb = seq // SLC_BLK
    return pl.pallas_call(
        _slc_kernel,
        grid=(batch, N_KV, nq),
        in_specs=[pl.BlockSpec((GROUP, HEAD_DIM, tq), lambda b, k, i: (k, 0, b * nq + i)),
                  pl.BlockSpec((GROUP, SUBLANE, tq), lambda b, k, i: (k, 0, b * nq + i)),
                  pl.BlockSpec((1, seq, LANE), lambda b, k, i: (k, b, 0)),
                  pl.BlockSpec((1, V_ROWS, seq), lambda b, k, i: (k, 0, b)),
                  pl.BlockSpec((1, 1, nb, tq), lambda b, k, i: (b, k, 0, i))],
        out_specs=pl.BlockSpec((tq, GROUP * HEAD_DIM), lambda b, k, i: (b * nq + i, k)),
        out_shape=jax.ShapeDtypeStruct((batch * seq, D_NSA), F32),
        compiler_params=_params(("parallel", "parallel", "parallel")),
        name="slcattn",
    )(qt, gt, ks, vst, bias)


def _win_kernel(qt_ref, gt_ref, k_ref, vt_ref, o_ref):
    tq = qt_ref.shape[2]
    sub = ATT_SUB
    n = pl.program_id(2)
    nback = WINDOW // tq
    qs = _attn_queries(qt_ref)
    tok = n * tq + lax.broadcasted_iota(jnp.int32, (1, 2 * tq), 1) % tq
    scores, vts, masks = [], [], []
    for back in range(nback + 1):
        kb = n - back
        start = pl.multiple_of(jnp.maximum(kb, 0) * tq, tq)
        for j in range(tq // sub):
            kc = k_ref[0, pl.ds(start + j * sub, sub), :]
            scores.append([jnp.dot(kc, q, preferred_element_type=F32) for q in qs])
            vts.append(vt_ref[0, :, pl.ds(start + j * sub, sub)])
            key = start + j * sub + lax.broadcasted_iota(jnp.int32, (sub, 2 * tq), 0)
            if back == 0:
                masks.append(key <= tok)
            elif back == nback:
                masks.append((tok - key < WINDOW) & (kb >= 0))
            else:
                masks.append(jnp.broadcast_to(kb >= 0, (sub, 2 * tq)))
    carry = _attn_pieces(scores, vts, masks, _attn_init(tq))
    _attn_store(o_ref, gt_ref, 2, _attn_finish(carry))


def _win(qt, gt, kw, vwt, batch, seq):
    tq = ATT_QTILE
    nq = seq // tq
    return pl.pallas_call(
        _win_kernel,
        grid=(batch, N_KV, nq),
        in_specs=[pl.BlockSpec((GROUP, HEAD_DIM, tq), lambda b, k, i: (k, 0, b * nq + i)),
                  pl.BlockSpec((GROUP, SUBLANE, tq), lambda b, k, i: (k, 0, b * nq + i)),
                  pl.BlockSpec((1, seq, HEAD_DIM), lambda b, k, i: (k, b, 0)),
                  pl.BlockSpec((1, V_ROWS, seq), lambda b, k, i: (k, 0, b))],
        out_specs=pl.BlockSpec((tq, GROUP * HEAD_DIM), lambda b, k, i: (b * nq + i, k)),
        out_shape=jax.ShapeDtypeStruct((batch * seq, D_NSA), F32),
        compiler_params=_params(("parallel", "parallel", "parallel")),
        name="winattn",
    )(qt, gt, kw, vwt)


def _outproj_kernel(x_ref, y_ref, gn_ref, oc_ref, os_ref, ow_ref, w_ref, o_ref):
    y_nsa = ((oc_ref[...] + os_ref[...]) + ow_ref[...]) * jax.nn.silu(gn_ref[...])
    acc = jnp.dot(y_ref[...], w_ref[0:D_LRU, :], preferred_element_type=F32)
    acc = acc + jnp.dot(y_nsa.astype(BF16), w_ref[D_LRU:, :], preferred_element_type=F32)
    o_ref[...] = x_ref[...] + acc


def _outproj(x2, ylru, gn, oc, osl, ow, w):
    m = x2.shape[0]
    tm = ROW_TILE
    row = lambda width: pl.BlockSpec((tm, width), lambda i: (i, 0))
    return pl.pallas_call(
        _outproj_kernel,
        grid=(m // tm,),
        in_specs=[row(D_MODEL), row(D_LRU), row(D_NSA), row(D_NSA), row(D_NSA), row(D_NSA),
                  pl.BlockSpec((D_MODEL, D_MODEL), lambda i: (0, 0))],
        out_specs=row(D_MODEL),
        out_shape=jax.ShapeDtypeStruct(x2.shape, F32),
        compiler_params=_params(("parallel",)),
        name="outproj",
    )(x2, ylru, gn, oc, osl, ow, w)


def _norm_kernel(x_ref, g_ref, o_ref):
    x = x_ref[...]
    ms = jnp.mean(x * x, axis=-1, keepdims=True)
    o_ref[...] = (x * lax.rsqrt(ms + EPS)) * g_ref[...]


def _final_norm(x2, g):
    m = x2.shape[0]
    tm = ROW_TILE
    row = pl.BlockSpec((tm, D_MODEL), lambda i: (i, 0))
    return pl.pallas_call(
        _norm_kernel,
        grid=(m // tm,),
        in_specs=[row, pl.BlockSpec((1, D_MODEL), lambda i: (0, 0))],
        out_specs=row,
        out_shape=jax.ShapeDtypeStruct(x2.shape, F32),
        compiler_params=_params(("parallel",)),
        name="finalnorm",
    )(x2, g)


def _rope_tables(seq):
    inv = 1.0 / (ROPE_THETA ** (jnp.arange(0, HEAD_DIM, 2, dtype=F32) / HEAD_DIM))
    ang = jnp.arange(seq, dtype=F32)[:, None] * inv[None, :]
    cos, sin = jnp.cos(ang), jnp.sin(ang)
    reps = LANE // HEAD_DIM
    cosf = jnp.tile(jnp.concatenate([cos, cos], axis=-1), (1, reps))
    sinf = jnp.tile(jnp.concatenate([-sin, sin], axis=-1), (1, reps))
    return cosf, sinf


def _block_diag(w):
    h, n, _ = w.shape
    eye = jnp.eye(h, dtype=w.dtype)
    return (eye[:, None, :, None] * w[:, :, None, :]).reshape(h * n, h * n)


def _layer(x2, batch, seq, cosf, sinf, g, w_in, conv_w, conv_b, wa, ba, wi, bi, lam,
           pos_k, pos_v, kw1, kw2, vw1, vw2, w_out):
    w_in_p = jnp.pad(w_in, ((0, 0), (0, D_IN_PAD - D_IN))).astype(BF16)
    (u, gl, gn, gt, kcr, vcr, qct, qt, ks, vst, kwn, vwt) = _inproj(
        x2, g.reshape(1, D_MODEL), w_in_p, cosf, sinf, seq)

    wg = jnp.concatenate([_block_diag(wa), _block_diag(wi)], axis=1).astype(BF16)
    ylru = _lru(u, gl, conv_w, conv_b.reshape(1, D_LRU), wg, ba.reshape(1, D_LRU),
                bi.reshape(1, D_LRU), lam.reshape(1, D_LRU), batch, seq)

    ratio = CMP_LEN // CMP_STRIDE
    kcmp, vcmpt = _compress(
        kcr, vcr, pos_k.reshape(ratio, SUB_FLAT), pos_v.reshape(ratio, SUB_FLAT),
        kw1.reshape(ratio, SUB_FLAT, CMP_HID).astype(BF16), kw2.astype(BF16),
        vw1.reshape(ratio, SUB_FLAT, CMP_HID).astype(BF16), vw2.T.astype(BF16), batch, seq)
    o_cmp, bias = _cmpsel(qct, gt, kcmp, vcmpt, batch, seq)
    o_slc = _slc(qt, gt, ks, vst, bias, batch, seq)
    o_win = _win(qt, gt, kwn, vwt, batch, seq)
    return _outproj(x2, ylru, gn, o_cmp, o_slc, o_win, w_out.astype(BF16))


def kernel(x, norm_g, w_in, conv_w, conv_b, lru_wa, lru_ba, lru_wi, lru_bi, lru_lambda, cmp_pos_k, cmp_pos_v, cmp_k_w1, cmp_k_w2, cmp_v_w1, cmp_v_w2, w_out, final_g):
    batch, seq, _ = x.shape
    assert seq % ROW_TILE == 0 and seq % SLC_CHUNK == 0 and (seq // SLC_BLK) % SUBLANE == 0
    cosf, sinf = _rope_tables(seq)
    h = x.reshape(batch * seq, D_MODEL)
    for l in range(norm_g.shape[0]):
        h = _layer(h, batch, seq, cosf, sinf, norm_g[l], w_in[l], conv_w[l], conv_b[l],
                   lru_wa[l], lru_ba[l], lru_wi[l], lru_bi[l], lru_lambda[l],
                   cmp_pos_k[l], cmp_pos_v[l], cmp_k_w1[l], cmp_k_w2[l],
                   cmp_v_w1[l], cmp_v_w2[l], w_out[l])
    return _final_norm(h, final_g.reshape(1, D_MODEL)).reshape(batch, seq, D_MODEL)
```

```python
import functools

import numpy as np
import jax
import jax.numpy as jnp
from jax import lax
from jax.experimental import pallas as pl
from jax.experimental.pallas import tpu as pltpu

F32 = jnp.float32
BF16 = jnp.bfloat16

D_MODEL = 1024
D_LRU = 512
H_LRU = 8
LRU_BLK = D_LRU // H_LRU
CONV_W = 4
LRU_C = 8.0
D_NSA = 512
HEAD_DIM = 64
N_HEADS = 8
N_KV = 2
GROUP = N_HEADS // N_KV
KV_W = N_KV * HEAD_DIM
CMP_LEN = 32
CMP_STRIDE = 16
CMP_HID = 256
SLC_BLK = 64
SLC_TOPK = 16
WINDOW = 512
ROPE_THETA = 10000.0
EPS = 1e-6
NEG = -1e30
FORCE_BONUS = 1e4
SCALE = HEAD_DIM ** -0.5
LOG2E = 1.4426950408889634
D_IN = 2 * D_LRU + 2 * D_NSA + 6 * KV_W + 3 * N_HEADS
LANE = 128
SUBLANE = 8
D_IN_PAD = ((D_IN + LANE - 1) // LANE) * LANE
SUB_FLAT = CMP_STRIDE * HEAD_DIM
N_BRANCH = 3

OFF_U = 0
OFF_GL = D_LRU
OFF_Q = 2 * D_LRU
OFF_GN = 2 * D_LRU + D_NSA
OFF_KV = 2 * D_LRU + 2 * D_NSA

ROW_TILE = 512
LRU_TILE = 256
ATT_QTILE = 256
SLC_CHUNK = 512
ATT_SUB = 128
V_ROWS = 80
VMEM_LIMIT = 48 * 1024 * 1024

_NT = (((1,), (1,)), ((), ()))


def _params(sem):
    return pltpu.CompilerParams(dimension_semantics=sem, vmem_limit_bytes=VMEM_LIMIT)


def _rope(t, cos, sin_signed):
    lane = lax.broadcasted_iota(jnp.int32, t.shape, 1)
    first = (lane % HEAD_DIM) < (HEAD_DIM // 2)
    rot = jnp.where(first, pltpu.roll(t, LANE - HEAD_DIM // 2, 1), pltpu.roll(t, HEAD_DIM // 2, 1))
    return t * cos + rot * sin_signed


def _inproj_body(x, g_ref, w_ref, cos_ref, sin_ref,
                 u_ref, gl_ref, gn_ref, gt_ref, kcr_ref, vcr_ref,
                 qct_ref, qt_ref, ks_ref, vst_ref, kw_ref, vwt_ref, kc_scr, vc_scr):
    tm = x.shape[0]
    ms = jnp.mean(x * x, axis=-1, keepdims=True)
    h = ((x * lax.rsqrt(ms + EPS)) * g_ref[...]).astype(BF16)
    cos = cos_ref[...]
    sin = sin_ref[...]

    def proj(off, width):
        return jnp.dot(h, w_ref[:, off:off + width], preferred_element_type=F32)

    u_ref[...] = proj(OFF_U, D_LRU)
    gl_ref[...] = proj(OFF_GL, D_LRU)
    gn_ref[...] = proj(OFF_GN, D_NSA)
    q = proj(OFF_Q, D_NSA)
    kvg = proj(OFF_KV, D_IN_PAD - OFF_KV)
    group = lambda i: kvg[:, i * KV_W:(i + 1) * KV_W]

    def put_heads_t(ref, base, slab_t):
        ref[base, 0:HEAD_DIM, :] = slab_t[:HEAD_DIM].astype(ref.dtype)
        ref[base + 1, 0:HEAD_DIM, :] = slab_t[HEAD_DIM:].astype(ref.dtype)

    for i in range(D_NSA // LANE):
        slab = q[:, i * LANE:(i + 1) * LANE]
        put_heads_t(qct_ref, 2 * i, (slab * (SCALE * LOG2E)).T)
        put_heads_t(qt_ref, 2 * i, (_rope(slab, cos, sin) * (SCALE * LOG2E)).T)

    for scr, dst, i in ((kc_scr, kcr_ref, 0), (vc_scr, vcr_ref, 1)):
        scr[...] = group(i)
        for j in range(CMP_STRIDE):
            rows = scr[pl.ds(j, tm // CMP_STRIDE, stride=CMP_STRIDE), :]
            for hh in range(N_KV):
                dst[hh, :, j * HEAD_DIM:(j + 1) * HEAD_DIM] = rows[:, hh * HEAD_DIM:(hh + 1) * HEAD_DIM]

    ks = _rope(group(2), cos, sin)
    lane = lax.broadcasted_iota(jnp.int32, (tm, LANE), 1)
    blk = (lax.broadcasted_iota(jnp.int32, (tm, LANE), 0) // SLC_BLK) % (SLC_CHUNK // SLC_BLK)
    onehot = (lane - HEAD_DIM == blk).astype(F32)
    ks_ref[0] = jnp.where(lane < HEAD_DIM, ks, onehot).astype(ks_ref.dtype)
    ks_ref[1] = jnp.where(lane < HEAD_DIM, pltpu.roll(ks, HEAD_DIM, 1), onehot).astype(ks_ref.dtype)
    kw = _rope(group(4), cos, sin)
    kw_ref[0] = kw[:, :HEAD_DIM].astype(kw_ref.dtype)
    kw_ref[1] = kw[:, HEAD_DIM:].astype(kw_ref.dtype)

    tail = (lax.broadcasted_iota(jnp.int32, (V_ROWS - HEAD_DIM, tm), 0) == 0).astype(BF16)
    for ref, i in ((vst_ref, 3), (vwt_ref, 5)):
        put_heads_t(ref, 0, group(i).T)
        for hh in range(N_KV):
            ref[hh, HEAD_DIM:V_ROWS, :] = tail

    gate_t = group(6).T
    fill = jnp.zeros((SUBLANE - N_BRANCH, tm), F32)
    for hd in range(N_HEADS):
        gt_ref[hd] = jnp.concatenate([gate_t[N_BRANCH * hd:N_BRANCH * (hd + 1)], fill], axis=0)


def _outproj_rows(x_ref, y_ref, gn_ref, oa_ref, wo_ref):
    y_nsa = oa_ref[...] * jax.nn.silu(gn_ref[...])
    acc = jnp.dot(y_ref[...], wo_ref[0:D_LRU, :], preferred_element_type=F32)
    acc = acc + jnp.dot(y_nsa.astype(BF16), wo_ref[D_LRU:, :], preferred_element_type=F32)
    return x_ref[...] + acc


def _inproj_kernel(x_ref, *refs):
    _inproj_body(x_ref[...], *refs)


def _outin_kernel(x_ref, y_ref, gn_ref, oa_ref, wo_ref, g_ref, w_ref, cos_ref, sin_ref, xo_ref, *refs):
    x = _outproj_rows(x_ref, y_ref, gn_ref, oa_ref, wo_ref)
    xo_ref[...] = x
    _inproj_body(x, g_ref, w_ref, cos_ref, sin_ref, *refs)


def _outnorm_kernel(x_ref, y_ref, gn_ref, oa_ref, wo_ref, g_ref, o_ref):
    x = _outproj_rows(x_ref, y_ref, gn_ref, oa_ref, wo_ref)
    ms = jnp.mean(x * x, axis=-1, keepdims=True)
    o_ref[...] = (x * lax.rsqrt(ms + EPS)) * g_ref[...]


def _const(shape):
    return pl.BlockSpec(shape, lambda i: (0,) * len(shape), pipeline_mode=pl.Buffered(1))


def _row(width):
    return pl.BlockSpec((ROW_TILE, width), lambda i: (i, 0))


def _outproj_specs():
    return [_row(D_MODEL), _row(D_LRU), _row(D_NSA), _row(D_NSA), _const((D_MODEL, D_MODEL))]


def _outnorm(x2, ylru, gn, oatt, wo, g):
    return pl.pallas_call(
        _outnorm_kernel,
        grid=(x2.shape[0] // ROW_TILE,),
        in_specs=_outproj_specs() + [_const((1, D_MODEL))],
        out_specs=_row(D_MODEL),
        out_shape=jax.ShapeDtypeStruct(x2.shape, F32),
        compiler_params=_params(("parallel",)),
        name="outnorm",
    )(x2, ylru, gn, oatt, wo, g)


def _inproj(x2, g, w, cosf, sinf, seq, prev=None):
    m = x2.shape[0]
    tm = ROW_TILE
    spt = seq // tm
    assert tm % SLC_CHUNK == 0
    row = _row
    heads = lambda n, width: pl.BlockSpec((n, tm, width), lambda i: (0, i, 0))
    heads_t = lambda n, rows: pl.BlockSpec((n, rows, tm), lambda i: (0, 0, i))
    flat = pl.BlockSpec((N_KV, tm // CMP_STRIDE, SUB_FLAT), lambda i: (0, i, 0))
    tab = pl.BlockSpec((tm, LANE), lambda i: (i % spt, 0))
    out_shape = (
        jax.ShapeDtypeStruct((m, D_LRU), F32),
        jax.ShapeDtypeStruct((m, D_LRU), F32),
        jax.ShapeDtypeStruct((m, D_NSA), F32),
        jax.ShapeDtypeStruct((N_HEADS, SUBLANE, m), F32),
        jax.ShapeDtypeStruct((N_KV, m // CMP_STRIDE, SUB_FLAT), F32),
        jax.ShapeDtypeStruct((N_KV, m // CMP_STRIDE, SUB_FLAT), F32),
        jax.ShapeDtypeStruct((N_HEADS, HEAD_DIM, m), BF16),
        jax.ShapeDtypeStruct((N_HEADS, HEAD_DIM, m), BF16),
        jax.ShapeDtypeStruct((N_KV, m, LANE), BF16),
        jax.ShapeDtypeStruct((N_KV, V_ROWS, m), BF16),
        jax.ShapeDtypeStruct((N_KV, m, HEAD_DIM), BF16),
        jax.ShapeDtypeStruct((N_KV, V_ROWS, m), BF16),
    )
    out_specs = (row(D_LRU), row(D_LRU), row(D_NSA), heads_t(N_HEADS, SUBLANE), flat, flat,
                 heads_t(N_HEADS, HEAD_DIM), heads_t(N_HEADS, HEAD_DIM), heads(N_KV, LANE), heads_t(N_KV, V_ROWS),
                 heads(N_KV, HEAD_DIM), heads_t(N_KV, V_ROWS))
    proj_specs = [_const((1, D_MODEL)), _const((D_MODEL, D_IN_PAD)), tab, tab]
    scratch = [pltpu.VMEM((tm, KV_W), F32)] * 2
    if prev is None:
        return pl.pallas_call(
            _inproj_kernel,
            grid=(m // tm,),
            in_specs=[row(D_MODEL)] + proj_specs,
            out_specs=out_specs,
            out_shape=out_shape,
            scratch_shapes=scratch,
            compiler_params=_params(("parallel",)),
            name="inproj",
        )(x2, g, w, cosf, sinf)
    ylru, gn, oatt, wo = prev
    return pl.pallas_call(
        _outin_kernel,
        grid=(m // tm,),
        in_specs=_outproj_specs() + proj_specs,
        out_specs=(row(D_MODEL),) + out_specs,
        out_shape=(jax.ShapeDtypeStruct(x2.shape, F32),) + out_shape,
        scratch_shapes=scratch,
        compiler_params=_params(("parallel",)),
        name="outin",
    )(x2, ylru, gn, oatt, wo, g, w, cosf, sinf)


def _softplus(x):
    return jnp.maximum(x, 0.0) + jnp.log1p(jnp.exp(-jnp.abs(x)))


def _lru_kernel(u_ref, g_ref, cw_ref, cb_ref, wg_ref, ba_ref, bi_ref, lam_ref, y_ref,
                ubuf, hprev, hseq):
    t = u_ref.shape[0]
    head = SUBLANE

    @pl.when(pl.program_id(1) == 0)
    def _():
        ubuf[0:head, :] = jnp.zeros((head, D_LRU), F32)
        hprev[...] = jnp.zeros_like(hprev)

    ubuf[head:head + t, :] = u_ref[...]
    conv = cb_ref[...]
    for k in range(CONV_W):
        conv = conv + cw_ref[k:k + 1, :] * ubuf[pl.ds(head - (CONV_W - 1) + k, t), :]
    ubuf[head - (CONV_W - 1):head, :] = ubuf[head + t - (CONV_W - 1):head + t, :]

    pre = jnp.dot(conv.astype(BF16), wg_ref[...], preferred_element_type=F32)
    r = jax.nn.sigmoid(pre[:, :D_LRU] + ba_ref[...])
    gi = jax.nn.sigmoid(pre[:, D_LRU:] + bi_ref[...])
    log_a = (-LRU_C * r) * _softplus(-lam_ref[...])
    a = jnp.exp(log_a)
    drive = jnp.sqrt((1.0 - a) * (1.0 + a)) * (gi * conv)

    ng = t // SUBLANE
    a3 = a.reshape(ng, SUBLANE, D_LRU)
    b3 = drive.reshape(ng, SUBLANE, D_LRU)
    sub = lax.broadcasted_iota(jnp.int32, (ng, SUBLANE, D_LRU), 1)
    d = 1
    while d < SUBLANE:
        keep = sub >= d
        b3 = jnp.where(keep, b3 + a3 * pltpu.roll(b3, d, 1), b3)
        a3 = jnp.where(keep, a3 * pltpu.roll(a3, d, 1), a3)
        d *= 2
    h_in = jnp.broadcast_to(hprev[...], (SUBLANE, D_LRU))
    for gidx in range(ng):
        h_g = a3[gidx] * h_in + b3[gidx]
        hseq[gidx * SUBLANE:(gidx + 1) * SUBLANE, :] = h_g
        h_in = jnp.broadcast_to(h_g[SUBLANE - 1:SUBLANE, :], (SUBLANE, D_LRU))
    hprev[...] = h_in[0:1, :]
    y_ref[...] = (hseq[...] * jax.nn.silu(g_ref[...])).astype(y_ref.dtype)


def _lru(u, gl, cw, cb, wg, ba, bi, lam, batch, seq):
    t = LRU_TILE
    spt = seq // t
    row = pl.BlockSpec((t, D_LRU), lambda b, j: (b * spt + j, 0))
    vec = pl.BlockSpec((1, D_LRU), lambda b, j: (0, 0))
    return pl.pallas_call(
        _lru_kernel,
        grid=(batch, spt),
        in_specs=[row, row,
                  pl.BlockSpec((CONV_W, D_LRU), lambda b, j: (0, 0)), vec,
                  pl.BlockSpec((D_LRU, 2 * D_LRU), lambda b, j: (0, 0)), vec, vec, vec],
        out_specs=row,
        out_shape=jax.ShapeDtypeStruct(u.shape, BF16),
        scratch_shapes=[pltpu.VMEM((SUBLANE + t, D_LRU), F32), pltpu.VMEM((1, D_LRU), F32),
                        pltpu.VMEM((t, D_LRU), F32)],
        compiler_params=_params(("parallel", "arbitrary")),
        name="rglru",
    )(u, gl, cw, cb, wg, ba, bi, lam)


def _compress_kernel(k_ref, v_ref, pk_ref, pv_ref, k1_ref, k2_ref, v1_ref, v2t_ref, ko_ref, vo_ref, shift):
    n = k_ref.shape[1]

    def hidden(x_ref, p_ref, w1_ref):
        x = x_ref[0]
        lo = jnp.dot((x + p_ref[0:1, :]).astype(BF16), w1_ref[0], preferred_element_type=F32)
        hi = jnp.dot((x + p_ref[1:2, :]).astype(BF16), w1_ref[1], preferred_element_type=F32)
        shift[0:n, :] = hi
        shift[n:n + 8, :] = jnp.zeros((8, CMP_HID), F32)
        return jax.nn.silu(lo + shift[pl.ds(1, n), :]).astype(BF16)

    ko_ref[0, 0] = jnp.dot(hidden(k_ref, pk_ref, k1_ref), k2_ref[...],
                           preferred_element_type=F32).astype(ko_ref.dtype)
    vo_ref[0, 0] = lax.dot_general(v2t_ref[...], hidden(v_ref, pv_ref, v1_ref), _NT,
                                   preferred_element_type=F32).astype(vo_ref.dtype)


def _compress(kcr, vcr, pk, pv, k1, k2, v1, v2t, batch, seq):
    n = seq // CMP_STRIDE
    xin = pl.BlockSpec((1, n, SUB_FLAT), lambda b, k: (k, b, 0))
    pos = pl.BlockSpec((2, SUB_FLAT), lambda b, k: (0, 0))
    w1 = pl.BlockSpec((2, SUB_FLAT, CMP_HID), lambda b, k: (0, 0, 0))
    return pl.pallas_call(
        _compress_kernel,
        grid=(batch, N_KV),
        in_specs=[xin, xin, pos, pos, w1, pl.BlockSpec((CMP_HID, HEAD_DIM), lambda b, k: (0, 0)),
                  w1, pl.BlockSpec((HEAD_DIM, CMP_HID), lambda b, k: (0, 0))],
        out_specs=(pl.BlockSpec((1, 1, n, HEAD_DIM), lambda b, k: (b, k, 0, 0)),
                   pl.BlockSpec((1, 1, HEAD_DIM, n), lambda b, k: (b, k, 0, 0))),
        out_shape=(jax.ShapeDtypeStruct((batch, N_KV, n, HEAD_DIM), BF16),
                   jax.ShapeDtypeStruct((batch, N_KV, HEAD_DIM, n), BF16)),
        scratch_shapes=[pltpu.VMEM((n + 8, CMP_HID), F32)],
        compiler_params=_params(("parallel", "parallel")),
        name="compress",
    )(kcr, vcr, pk, pv, k1, k2, v1, v2t)


def _attn_queries(qt_ref):
    return [jnp.concatenate([qt_ref[2 * p], qt_ref[2 * p + 1]], axis=1) for p in range(GROUP // 2)]


def _attn_store(o_ref, gt_ref, branches):
    for p in range(GROUP // 2):
        tq = branches[0][p].shape[1] // 2
        halves = []
        for e in range(2):
            cols = slice(e * tq, (e + 1) * tq)
            terms = [jax.nn.sigmoid(gt_ref[2 * p + e, br:br + 1, :]) * outs[p][:, cols]
                     for br, outs in enumerate(branches)]
            halves.append((terms[0] + terms[1]) + terms[2])
        o_ref[:, p * 2 * HEAD_DIM:(p + 1) * 2 * HEAD_DIM] = jnp.concatenate(halves, axis=0).T


def _attn_pieces(scores, vts, masks, carry):
    state = list(carry)
    for j, vt in enumerate(vts):
        for p, (m_i, acc) in enumerate(state):
            st = scores[j][p]
            if masks[j] is not None:
                st = jnp.where(masks[j], st, NEG)
            m_new = jnp.maximum(m_i, jnp.max(st, axis=0, keepdims=True))
            pt = jnp.exp2(st - m_new).astype(BF16)
            acc = jnp.exp2(m_i - m_new) * acc + jnp.dot(vt, pt, preferred_element_type=F32)
            state[p] = (m_new, acc)
    return tuple(state)


def _attn_init(tq):
    one = (jnp.full((1, 2 * tq), NEG, F32), jnp.zeros((V_ROWS, 2 * tq), F32))
    return (one,) * (GROUP // 2)


def _attn_finish(carry):
    return [acc[:HEAD_DIM] / acc[HEAD_DIM:HEAD_DIM + 1] for _, acc in carry]


def _split3(x):
    hi = x.astype(BF16)
    r1 = x - hi.astype(F32)
    mid = r1.astype(BF16)
    lo = (r1 - mid.astype(F32)).astype(BF16)
    return hi, mid, lo


def _cmp_branch(qt_ref, kc_ref, vct_ref, mt_ref, bias_ref, t0, tok):
    tq = qt_ref.shape[2]
    ncp = kc_ref.shape[2]
    nb = mt_ref.shape[0]
    kc = kc_ref[0, 0]
    vct = vct_ref[0, 0]
    blk_end = lax.broadcasted_iota(jnp.int32, (ncp, 2 * tq), 0) * CMP_STRIDE + (CMP_LEN - 1)
    valid = blk_end <= tok
    any_valid = tok >= CMP_LEN - 1
    psum = jnp.zeros((ncp, tq), F32)
    outs = []
    for q in _attn_queries(qt_ref):
        st = jnp.where(valid, jnp.dot(kc, q, preferred_element_type=F32), NEG)
        e = jnp.exp2(st - jnp.max(st, axis=0, keepdims=True))
        inv = jnp.where(any_valid, 1.0 / jnp.sum(e, axis=0, keepdims=True), 0.0)
        p = e * inv
        outs.append(jnp.dot(vct, p.astype(BF16), preferred_element_type=F32))
        psum = psum + (p[:, :tq] + p[:, tq:])

    mt = mt_ref[...]
    sc = jnp.zeros((nb, tq), F32)
    for part in _split3(psum):
        sc = sc + jnp.dot(mt, part, preferred_element_type=F32)
    jj = lax.broadcasted_iota(jnp.int32, (nb, tq), 0)
    cur = (t0 + lax.broadcasted_iota(jnp.int32, (nb, tq), 1)) // SLC_BLK
    forced = (jj == 0) | (jj == cur) | (jj == cur - 1)
    sc = jnp.where(forced, sc + FORCE_BONUS, sc)
    sc = jnp.where(jj <= cur, sc, NEG)

    ngrp = nb // SUBLANE
    rows = [sc[g * SUBLANE:(g + 1) * SUBLANE] for g in range(ngrp)]
    cnt = [jnp.zeros((SUBLANE, tq), jnp.int32) for _ in range(ngrp)]
    sub = lax.broadcasted_iota(jnp.int32, (SUBLANE, tq), 0)
    for gp in range(ngrp):
        for r in range(SUBLANE):
            other = jnp.broadcast_to(rows[gp][r:r + 1], (SUBLANE, tq))
            for g in range(ngrp):
                if g < gp:
                    before = other > rows[g]
                elif g > gp:
                    before = other >= rows[g]
                else:
                    before = (other > rows[g]) | ((other == rows[g]) & (sub > r))
                cnt[g] = cnt[g] + before.astype(jnp.int32)
    for g in range(ngrp):
        selected = (cnt[g] < min(SLC_TOPK, nb)) & (rows[g] > NEG / 2)
        bias_ref[g * SUBLANE:(g + 1) * SUBLANE, :] = jnp.where(selected, 0.0, NEG)
    return outs


def _overlap_t(seq):
    nc_pad = seq // CMP_STRIDE
    n_slc = seq // SLC_BLK
    c0 = np.arange(nc_pad)[None, :] * CMP_STRIDE
    j0 = np.arange(n_slc)[:, None] * SLC_BLK
    ov = np.clip(np.minimum(c0 + CMP_LEN, j0 + SLC_BLK) - np.maximum(c0, j0), 0, None) / CMP_LEN
    ov[:, nc_pad - 1] = 0.0
    return jnp.asarray(ov, dtype=BF16)


def _slc_branch(qs, k_ref, vt_ref, bias_ref, i, tok):
    tq = tok.shape[1] // 2
    ck = SLC_CHUNK
    sub = ATT_SUB
    bpc = ck // SLC_BLK
    zrows = jnp.zeros((LANE - HEAD_DIM - 2 * bpc, 2 * tq), BF16)

    def chunk(c, carry, causal):
        start = pl.multiple_of(c * ck, ck)
        b8 = bias_ref[pl.ds(pl.multiple_of(c * bpc, bpc), bpc), :]
        b16 = jnp.concatenate([b8, jnp.zeros_like(b8)], axis=0).astype(BF16)
        brows = jnp.concatenate([b16, b16], axis=1)
        rhs = [jnp.concatenate([q, brows, zrows], axis=0) for q in qs]
        scores, vts, masks = [], [], []
        for j in range(ck // sub):
            kc = k_ref[0, pl.ds(start + j * sub, sub), :]
            scores.append([jnp.dot(kc, r, preferred_element_type=F32) for r in rhs])
            vts.append(vt_ref[0, :, pl.ds(start + j * sub, sub)])
            key = start + j * sub + lax.broadcasted_iota(jnp.int32, (sub, 2 * tq), 0)
            masks.append(key <= tok if causal else None)
        return _attn_pieces(scores, vts, masks, carry)

    n_chunks = (i * tq + tq + ck - 1) // ck
    carry = lax.fori_loop(0, n_chunks - 1, functools.partial(chunk, causal=False), _attn_init(tq))
    return _attn_finish(chunk(n_chunks - 1, carry, causal=True))


def _win_branch(qs, k_ref, vt_ref, n, tok):
    tq = tok.shape[1] // 2
    sub = ATT_SUB
    nback = WINDOW // tq
    scores, vts, masks = [], [], []
    for back in range(nback + 1):
        kb = n - back
        start = pl.multiple_of(jnp.maximum(kb, 0) * tq, tq)
        for j in range(tq // sub):
            kc = k_ref[0, pl.ds(start + j * sub, sub), :]
            scores.append([jnp.dot(kc, q, preferred_element_type=F32) for q in qs])
            vts.append(vt_ref[0, :, pl.ds(start + j * sub, sub)])
            key = start + j * sub + lax.broadcasted_iota(jnp.int32, (sub, 2 * tq), 0)
            if back == 0:
                masks.append(key <= tok)
            elif back == nback:
                masks.append((tok - key < WINDOW) & (kb >= 0))
            else:
                masks.append(jnp.broadcast_to(kb >= 0, (sub, 2 * tq)))
    return _attn_finish(_attn_pieces(scores, vts, masks, _attn_init(tq)))


def _nsa_kernel(qct_ref, qt_ref, gt_ref, kc_ref, vct_ref, mt_ref, ks_ref, vst_ref, kw_ref, vwt_ref,
                o_ref, bias_scr):
    tq = qt_ref.shape[2]
    i = pl.program_id(2)
    tok = i * tq + lax.broadcasted_iota(jnp.int32, (1, 2 * tq), 1) % tq
    qs = _attn_queries(qt_ref)
    o_cmp = _cmp_branch(qct_ref, kc_ref, vct_ref, mt_ref, bias_scr, i * tq, tok)
    o_win = _win_branch(qs, kw_ref, vwt_ref, i, tok)
    o_slc = _slc_branch(qs, ks_ref, vst_ref, bias_scr, i, tok)
    _attn_store(o_ref, gt_ref, (o_cmp, o_slc, o_win))


def _nsa(qct, qt, gt, kcmp, vcmpt, ks, vst, kw, vwt, batch, seq):
    tq = ATT_QTILE
    nq = seq // tq
    ncp = seq // CMP_STRIDE
    nb = seq // SLC_BLK
    qspec = pl.BlockSpec((GROUP, HEAD_DIM, tq), lambda b, k, i: (k, 0, b * nq + i))
    vspec = pl.BlockSpec((1, V_ROWS, seq), lambda b, k, i: (k, 0, b))
    return pl.pallas_call(
        _nsa_kernel,
        grid=(batch, N_KV, nq),
        in_specs=[qspec, qspec,
                  pl.BlockSpec((GROUP, SUBLANE, tq), lambda b, k, i: (k, 0, b * nq + i)),
                  pl.BlockSpec((1, 1, ncp, HEAD_DIM), lambda b, k, i: (b, k, 0, 0)),
                  pl.BlockSpec((1, 1, HEAD_DIM, ncp), lambda b, k, i: (b, k, 0, 0)),
                  pl.BlockSpec((nb, ncp), lambda b, k, i: (0, 0)),
                  pl.BlockSpec((1, seq, LANE), lambda b, k, i: (k, b, 0)), vspec,
                  pl.BlockSpec((1, seq, HEAD_DIM), lambda b, k, i: (k, b, 0)), vspec],
        out_specs=pl.BlockSpec((tq, GROUP * HEAD_DIM), lambda b, k, i: (b * nq + i, k)),
        out_shape=jax.ShapeDtypeStruct((batch * seq, D_NSA), F32),
        scratch_shapes=[pltpu.VMEM((nb, tq), F32)],
        compiler_params=_params(("parallel", "parallel", "parallel")),
        name="nsa",
    )(qct, qt, gt, kcmp, vcmpt, _overlap_t(seq), ks, vst, kw, vwt)


def _rope_tables(seq):
    inv = 1.0 / (ROPE_THETA ** (jnp.arange(0, HEAD_DIM, 2, dtype=F32) / HEAD_DIM))
    ang = jnp.arange(seq, dtype=F32)[:, None] * inv[None, :]
    cos, sin = jnp.cos(ang), jnp.sin(ang)
    reps = LANE // HEAD_DIM
    cosf = jnp.tile(jnp.concatenate([cos, cos], axis=-1), (1, reps))
    sinf = jnp.tile(jnp.concatenate([-sin, sin], axis=-1), (1, reps))
    return cosf, sinf


def _block_diag(w):
    h, n, _ = w.shape
    eye = jnp.eye(h, dtype=w.dtype)
    return (eye[:, None, :, None] * w[:, :, None, :]).reshape(h * n, h * n)


def _mixers(proj, batch, seq, conv_w, conv_b, wa, ba, wi, bi, lam, pos_k, pos_v, kw1, kw2, vw1, vw2):
    (u, gl, gn, gt, kcr, vcr, qct, qt, ks, vst, kwn, vwt) = proj
    wg = jnp.concatenate([_block_diag(wa), _block_diag(wi)], axis=1).astype(BF16)
    ylru = _lru(u, gl, conv_w, conv_b.reshape(1, D_LRU), wg, ba.reshape(1, D_LRU),
                bi.reshape(1, D_LRU), lam.reshape(1, D_LRU), batch, seq)
    ratio = CMP_LEN // CMP_STRIDE
    kcmp, vcmpt = _compress(
        kcr, vcr, pos_k.reshape(ratio, SUB_FLAT), pos_v.reshape(ratio, SUB_FLAT),
        kw1.reshape(ratio, SUB_FLAT, CMP_HID).astype(BF16), kw2.astype(BF16),
        vw1.reshape(ratio, SUB_FLAT, CMP_HID).astype(BF16), vw2.T.astype(BF16), batch, seq)
    return ylru, gn, _nsa(qct, qt, gt, kcmp, vcmpt, ks, vst, kwn, vwt, batch, seq)


def kernel(x, norm_g, w_in, conv_w, conv_b, lru_wa, lru_ba, lru_wi, lru_bi, lru_lambda, cmp_pos_k, cmp_pos_v, cmp_k_w1, cmp_k_w2, cmp_v_w1, cmp_v_w2, w_out, final_g):
    batch, seq, _ = x.shape
    assert seq % ROW_TILE == 0 and seq % SLC_CHUNK == 0 and (seq // SLC_BLK) % SUBLANE == 0
    cosf, sinf = _rope_tables(seq)
    depth = norm_g.shape[0]
    w_in_p = jnp.pad(w_in, ((0, 0), (0, 0), (0, D_IN_PAD - D_IN))).astype(BF16)
    w_out_b = w_out.astype(BF16)
    h = x.reshape(batch * seq, D_MODEL)
    prev = None
    for l in range(depth):
        proj = _inproj(h, norm_g[l].reshape(1, D_MODEL), w_in_p[l], cosf, sinf, seq, prev)
        if prev is not None:
            h, proj = proj[0], proj[1:]
        ylru, gn, yatt = _mixers(proj, batch, seq, conv_w[l], conv_b[l], lru_wa[l], lru_ba[l], lru_wi[l],
                                 lru_bi[l], lru_lambda[l], cmp_pos_k[l], cmp_pos_v[l], cmp_k_w1[l],
                                 cmp_k_w2[l], cmp_v_w1[l], cmp_v_w2[l])
        prev = (ylru, gn, yatt, w_out_b[l])
    out = _outnorm(h, *prev, final_g.reshape(1, D_MODEL))
    return out.reshape(batch, seq, D_MODEL)
```

```python
import functools

import numpy as np
import jax
import jax.numpy as jnp
from jax import lax
from jax.experimental import pallas as pl
from jax.experimental.pallas import tpu as pltpu

F32 = jnp.float32
BF16 = jnp.bfloat16

D_MODEL = 1024
D_LRU = 512
H_LRU = 8
LRU_BLK = D_LRU // H_LRU
CONV_W = 4
LRU_C = 8.0
D_NSA = 512
HEAD_DIM = 64
N_HEADS = 8
N_KV = 2
GROUP = N_HEADS // N_KV
KV_W = N_KV * HEAD_DIM
CMP_LEN = 32
CMP_STRIDE = 16
CMP_HID = 256
SLC_BLK = 64
SLC_TOPK = 16
WINDOW = 512
ROPE_THETA = 10000.0
EPS = 1e-6
NEG = -1e30
FORCE_BONUS = 1e4
SCALE = HEAD_DIM ** -0.5
LOG2E = 1.4426950408889634
D_IN = 2 * D_LRU + 2 * D_NSA + 6 * KV_W + 3 * N_HEADS
LANE = 128
SUBLANE = 8
D_IN_PAD = ((D_IN + LANE - 1) // LANE) * LANE
SUB_FLAT = CMP_STRIDE * HEAD_DIM
N_BRANCH = 3

OFF_U = 0
OFF_GL = D_LRU
OFF_Q = 2 * D_LRU
OFF_GN = 2 * D_LRU + D_NSA
OFF_KV = 2 * D_LRU + 2 * D_NSA

ROW_TILE = 512
LRU_TILE = 256
ATT_QTILE = 256
SLC_CHUNK = 512
ATT_SUB = 128
V_ROWS = 80
VMEM_LIMIT = 48 * 1024 * 1024

_NT = (((1,), (1,)), ((), ()))


def _params(sem):
    return pltpu.CompilerParams(dimension_semantics=sem, vmem_limit_bytes=VMEM_LIMIT)


def _rope(t, cos, sin_signed):
    lane = lax.broadcasted_iota(jnp.int32, t.shape, 1)
    first = (lane % HEAD_DIM) < (HEAD_DIM // 2)
    rot = jnp.where(first, pltpu.roll(t, LANE - HEAD_DIM // 2, 1), pltpu.roll(t, HEAD_DIM // 2, 1))
    return t * cos + rot * sin_signed


def _inproj_body(x, g_ref, w_ref, cos_ref, sin_ref,
                 u_ref, gl_ref, gn_ref, gt_ref, kcr_ref, vcr_ref,
                 qct_ref, qt_ref, ks_ref, vst_ref, kw_ref, vwt_ref, kc_scr, vc_scr):
    tm = x.shape[0]
    ms = jnp.mean(x * x, axis=-1, keepdims=True)
    h = ((x * lax.rsqrt(ms + EPS)) * g_ref[...]).astype(BF16)
    cos = cos_ref[...]
    sin = sin_ref[...]

    def proj(off, width):
        return jnp.dot(h, w_ref[:, off:off + width], preferred_element_type=F32)

    u_ref[...] = proj(OFF_U, D_LRU)
    gl_ref[...] = proj(OFF_GL, D_LRU)
    gn_ref[...] = proj(OFF_GN, D_NSA)
    q = proj(OFF_Q, D_NSA)
    kvg = proj(OFF_KV, D_IN_PAD - OFF_KV)
    group = lambda i: kvg[:, i * KV_W:(i + 1) * KV_W]

    def put_heads_t(ref, base, slab_t):
        ref[base, 0:HEAD_DIM, :] = slab_t[:HEAD_DIM].astype(ref.dtype)
        ref[base + 1, 0:HEAD_DIM, :] = slab_t[HEAD_DIM:].astype(ref.dtype)

    for i in range(D_NSA // LANE):
        slab = q[:, i * LANE:(i + 1) * LANE]
        put_heads_t(qct_ref, 2 * i, (slab * (SCALE * LOG2E)).T)
        put_heads_t(qt_ref, 2 * i, (_rope(slab, cos, sin) * (SCALE * LOG2E)).T)

    for scr, dst, i in ((kc_scr, kcr_ref, 0), (vc_scr, vcr_ref, 1)):
        scr[...] = group(i)
        for j in range(CMP_STRIDE):
            rows = scr[pl.ds(j, tm // CMP_STRIDE, stride=CMP_STRIDE), :]
            for hh in range(N_KV):
                dst[hh, :, j * HEAD_DIM:(j + 1) * HEAD_DIM] = rows[:, hh * HEAD_DIM:(hh + 1) * HEAD_DIM]

    ks = _rope(group(2), cos, sin)
    lane = lax.broadcasted_iota(jnp.int32, (tm, LANE), 1)
    blk = (lax.broadcasted_iota(jnp.int32, (tm, LANE), 0) // SLC_BLK) % (SLC_CHUNK // SLC_BLK)
    onehot = (lane - HEAD_DIM == blk).astype(F32)
    ks_ref[0] = jnp.where(lane < HEAD_DIM, ks, onehot).astype(ks_ref.dtype)
    ks_ref[1] = jnp.where(lane < HEAD_DIM, pltpu.roll(ks, HEAD_DIM, 1), onehot).astype(ks_ref.dtype)
    kw = _rope(group(4), cos, sin)
    kw_ref[0] = kw[:, :HEAD_DIM].astype(kw_ref.dtype)
    kw_ref[1] = kw[:, HEAD_DIM:].astype(kw_ref.dtype)

    tail = (lax.broadcasted_iota(jnp.int32, (V_ROWS - HEAD_DIM, tm), 0) == 0).astype(BF16)
    for ref, i in ((vst_ref, 3), (vwt_ref, 5)):
        put_heads_t(ref, 0, group(i).T)
        for hh in range(N_KV):
            ref[hh, HEAD_DIM:V_ROWS, :] = tail

    gate_t = group(6).T
    fill = jnp.zeros((SUBLANE - N_BRANCH, tm), F32)
    for hd in range(N_HEADS):
        gt_ref[hd] = jnp.concatenate([gate_t[N_BRANCH * hd:N_BRANCH * (hd + 1)], fill], axis=0)


def _outproj_rows(x_ref, y_ref, gn_ref, oa_ref, wo_ref):
    y_nsa = oa_ref[...] * jax.nn.silu(gn_ref[...])
    acc = jnp.dot(y_ref[...], wo_ref[0:D_LRU, :], preferred_element_type=F32)
    acc = acc + jnp.dot(y_nsa.astype(BF16), wo_ref[D_LRU:, :], preferred_element_type=F32)
    return x_ref[...] + acc


def _inproj_kernel(x_ref, *refs):
    _inproj_body(x_ref[...], *refs)


def _outin_kernel(x_ref, y_ref, gn_ref, oa_ref, wo_ref, g_ref, w_ref, cos_ref, sin_ref, xo_ref, *refs):
    x = _outproj_rows(x_ref, y_ref, gn_ref, oa_ref, wo_ref)
    xo_ref[...] = x
    _inproj_body(x, g_ref, w_ref, cos_ref, sin_ref, *refs)


def _outnorm_kernel(x_ref, y_ref, gn_ref, oa_ref, wo_ref, g_ref, o_ref):
    x = _outproj_rows(x_ref, y_ref, gn_ref, oa_ref, wo_ref)
    ms = jnp.mean(x * x, axis=-1, keepdims=True)
    o_ref[...] = (x * lax.rsqrt(ms + EPS)) * g_ref[...]


def _const(shape):
    return pl.BlockSpec(shape, lambda i: (0,) * len(shape), pipeline_mode=pl.Buffered(1))


def _row(width):
    return pl.BlockSpec((ROW_TILE, width), lambda i: (i, 0))


def _outproj_specs():
    return [_row(D_MODEL), _row(D_LRU), _row(D_NSA), _row(D_NSA), _const((D_MODEL, D_MODEL))]


def _outnorm(x2, ylru, gn, oatt, wo, g):
    return pl.pallas_call(
        _outnorm_kernel,
        grid=(x2.shape[0] // ROW_TILE,),
        in_specs=_outproj_specs() + [_const((1, D_MODEL))],
        out_specs=_row(D_MODEL),
        out_shape=jax.ShapeDtypeStruct(x2.shape, F32),
        compiler_params=_params(("parallel",)),
        name="outnorm",
    )(x2, ylru, gn, oatt, wo, g)


def _inproj(x2, g, w, cosf, sinf, seq, prev=None):
    m = x2.shape[0]
    tm = ROW_TILE
    spt = seq // tm
    assert tm % SLC_CHUNK == 0
    row = _row
    heads = lambda n, width: pl.BlockSpec((n, tm, width), lambda i: (0, i, 0))
    heads_t = lambda n, rows: pl.BlockSpec((n, rows, tm), lambda i: (0, 0, i))
    flat = pl.BlockSpec((N_KV, tm // CMP_STRIDE, SUB_FLAT), lambda i: (0, i, 0))
    tab = pl.BlockSpec((tm, LANE), lambda i: (i % spt, 0))
    out_shape = (
        jax.ShapeDtypeStruct((m, D_LRU), F32),
        jax.ShapeDtypeStruct((m, D_LRU), F32),
        jax.ShapeDtypeStruct((m, D_NSA), F32),
        jax.ShapeDtypeStruct((N_HEADS, SUBLANE, m), F32),
        jax.ShapeDtypeStruct((N_KV, m // CMP_STRIDE, SUB_FLAT), F32),
        jax.ShapeDtypeStruct((N_KV, m // CMP_STRIDE, SUB_FLAT), F32),
        jax.ShapeDtypeStruct((N_HEADS, HEAD_DIM, m), BF16),
        jax.ShapeDtypeStruct((N_HEADS, HEAD_DIM, m), BF16),
        jax.ShapeDtypeStruct((N_KV, m, LANE), BF16),
        jax.ShapeDtypeStruct((N_KV, V_ROWS, m), BF16),
        jax.ShapeDtypeStruct((N_KV, m, HEAD_DIM), BF16),
        jax.ShapeDtypeStruct((N_KV, V_ROWS, m), BF16),
    )
    out_specs = (row(D_LRU), row(D_LRU), row(D_NSA), heads_t(N_HEADS, SUBLANE), flat, flat,
                 heads_t(N_HEADS, HEAD_DIM), heads_t(N_HEADS, HEAD_DIM), heads(N_KV, LANE), heads_t(N_KV, V_ROWS),
                 heads(N_KV, HEAD_DIM), heads_t(N_KV, V_ROWS))
    proj_specs = [_const((1, D_MODEL)), _const((D_MODEL, D_IN_PAD)), tab, tab]
    scratch = [pltpu.VMEM((tm, KV_W), F32)] * 2
    if prev is None:
        return pl.pallas_call(
            _inproj_kernel,
            grid=(m // tm,),
            in_specs=[row(D_MODEL)] + proj_specs,
            out_specs=out_specs,
            out_shape=out_shape,
            scratch_shapes=scratch,
            compiler_params=_params(("parallel",)),
            name="inproj",
        )(x2, g, w, cosf, sinf)
    ylru, gn, oatt, wo = prev
    return pl.pallas_call(
        _outin_kernel,
        grid=(m // tm,),
        in_specs=_outproj_specs() + proj_specs,
        out_specs=(row(D_MODEL),) + out_specs,
        out_shape=(jax.ShapeDtypeStruct(x2.shape, F32),) + out_shape,
        scratch_shapes=scratch,
        compiler_params=_params(("parallel",)),
        name="outin",
    )(x2, ylru, gn, oatt, wo, g, w, cosf, sinf)


def _softplus(x):
    return jnp.maximum(x, 0.0) + jnp.log1p(jnp.exp(-jnp.abs(x)))


def _lru_kernel(u_ref, g_ref, cw_ref, cb_ref, wg_ref, ba_ref, bi_ref, lam_ref, y_ref,
                ubuf, hprev, hseq):
    t = u_ref.shape[0]
    head = SUBLANE

    @pl.when(pl.program_id(1) == 0)
    def _():
        ubuf[0:head, :] = jnp.zeros((head, D_LRU), F32)
        hprev[...] = jnp.zeros_like(hprev)

    ubuf[head:head + t, :] = u_ref[...]
    conv = cb_ref[...]
    for k in range(CONV_W):
        conv = conv + cw_ref[k:k + 1, :] * ubuf[pl.ds(head - (CONV_W - 1) + k, t), :]
    ubuf[head - (CONV_W - 1):head, :] = ubuf[head + t - (CONV_W - 1):head + t, :]

    pre = jnp.dot(conv.astype(BF16), wg_ref[...], preferred_element_type=F32)
    r = jax.nn.sigmoid(pre[:, :D_LRU] + ba_ref[...])
    gi = jax.nn.sigmoid(pre[:, D_LRU:] + bi_ref[...])
    log_a = (-LRU_C * r) * _softplus(-lam_ref[...])
    a = jnp.exp(log_a)
    drive = jnp.sqrt((1.0 - a) * (1.0 + a)) * (gi * conv)

    ng = t // SUBLANE
    a3 = a.reshape(ng, SUBLANE, D_LRU)
    b3 = drive.reshape(ng, SUBLANE, D_LRU)
    sub = lax.broadcasted_iota(jnp.int32, (ng, SUBLANE, D_LRU), 1)
    d = 1
    while d < SUBLANE:
        keep = sub >= d
        b3 = jnp.where(keep, b3 + a3 * pltpu.roll(b3, d, 1), b3)
        a3 = jnp.where(keep, a3 * pltpu.roll(a3, d, 1), a3)
        d *= 2
    h_in = jnp.broadcast_to(hprev[...], (SUBLANE, D_LRU))
    for gidx in range(ng):
        h_g = a3[gidx] * h_in + b3[gidx]
        hseq[gidx * SUBLANE:(gidx + 1) * SUBLANE, :] = h_g
        h_in = jnp.broadcast_to(h_g[SUBLANE - 1:SUBLANE, :], (SUBLANE, D_LRU))
    hprev[...] = h_in[0:1, :]
    y_ref[...] = (hseq[...] * jax.nn.silu(g_ref[...])).astype(y_ref.dtype)


def _lru(u, gl, cw, cb, wg, ba, bi, lam, batch, seq):
    t = LRU_TILE
    spt = seq // t
    row = pl.BlockSpec((t, D_LRU), lambda b, j: (b * spt + j, 0))
    vec = pl.BlockSpec((1, D_LRU), lambda b, j: (0, 0))
    return pl.pallas_call(
        _lru_kernel,
        grid=(batch, spt),
        in_specs=[row, row,
                  pl.BlockSpec((CONV_W, D_LRU), lambda b, j: (0, 0)), vec,
                  pl.BlockSpec((D_LRU, 2 * D_LRU), lambda b, j: (0, 0)), vec, vec, vec],
        out_specs=row,
        out_shape=jax.ShapeDtypeStruct(u.shape, BF16),
        scratch_shapes=[pltpu.VMEM((SUBLANE + t, D_LRU), F32), pltpu.VMEM((1, D_LRU), F32),
                        pltpu.VMEM((t, D_LRU), F32)],
        compiler_params=_params(("parallel", "arbitrary")),
        name="rglru",
    )(u, gl, cw, cb, wg, ba, bi, lam)


def _compress_kernel(k_ref, v_ref, pk_ref, pv_ref, k1_ref, k2_ref, v1_ref, v2t_ref, ko_ref, vo_ref, shift):
    n = k_ref.shape[1]

    def hidden(x_ref, p_ref, w1_ref):
        x = x_ref[0]
        lo = jnp.dot((x + p_ref[0:1, :]).astype(BF16), w1_ref[0], preferred_element_type=F32)
        hi = jnp.dot((x + p_ref[1:2, :]).astype(BF16), w1_ref[1], preferred_element_type=F32)
        shift[0:n, :] = hi
        shift[n:n + 8, :] = jnp.zeros((8, CMP_HID), F32)
        return jax.nn.silu(lo + shift[pl.ds(1, n), :]).astype(BF16)

    ko_ref[0, 0] = jnp.dot(hidden(k_ref, pk_ref, k1_ref), k2_ref[...],
                           preferred_element_type=F32).astype(ko_ref.dtype)
    vo_ref[0, 0] = lax.dot_general(v2t_ref[...], hidden(v_ref, pv_ref, v1_ref), _NT,
                                   preferred_element_type=F32).astype(vo_ref.dtype)


def _compress(kcr, vcr, pk, pv, k1, k2, v1, v2t, batch, seq):
    n = seq // CMP_STRIDE
    xin = pl.BlockSpec((1, n, SUB_FLAT), lambda b, k: (k, b, 0))
    pos = pl.BlockSpec((2, SUB_FLAT), lambda b, k: (0, 0))
    w1 = pl.BlockSpec((2, SUB_FLAT, CMP_HID), lambda b, k: (0, 0, 0))
    return pl.pallas_call(
        _compress_kernel,
        grid=(batch, N_KV),
        in_specs=[xin, xin, pos, pos, w1, pl.BlockSpec((CMP_HID, HEAD_DIM), lambda b, k: (0, 0)),
                  w1, pl.BlockSpec((HEAD_DIM, CMP_HID), lambda b, k: (0, 0))],
        out_specs=(pl.BlockSpec((1, 1, n, HEAD_DIM), lambda b, k: (b, k, 0, 0)),
                   pl.BlockSpec((1, 1, HEAD_DIM, n), lambda b, k: (b, k, 0, 0))),
        out_shape=(jax.ShapeDtypeStruct((batch, N_KV, n, HEAD_DIM), BF16),
                   jax.ShapeDtypeStruct((batch, N_KV, HEAD_DIM, n), BF16)),
        scratch_shapes=[pltpu.VMEM((n + 8, CMP_HID), F32)],
        compiler_params=_params(("parallel", "parallel")),
        name="compress",
    )(kcr, vcr, pk, pv, k1, k2, v1, v2t)


def _attn_queries(qt_ref):
    return [jnp.concatenate([qt_ref[2 * p], qt_ref[2 * p + 1]], axis=1) for p in range(GROUP // 2)]


def _attn_store(o_ref, gt_ref, branches):
    for p in range(GROUP // 2):
        tq = branches[0][p].shape[1] // 2
        halves = []
        for e in range(2):
            cols = slice(e * tq, (e + 1) * tq)
            terms = [jax.nn.sigmoid(gt_ref[2 * p + e, br:br + 1, :]) * outs[p][:, cols]
                     for br, outs in enumerate(branches)]
            halves.append((terms[0] + terms[1]) + terms[2])
        o_ref[:, p * 2 * HEAD_DIM:(p + 1) * 2 * HEAD_DIM] = jnp.concatenate(halves, axis=0).T


def _attn_pieces(scores, vts, masks, carry):
    state = list(carry)
    for j, vt in enumerate(vts):
        for p, (m_i, acc) in enumerate(state):
            st = scores[j][p]
            if masks[j] is not None:
                st = jnp.where(masks[j], st, NEG)
            m_new = jnp.maximum(m_i, jnp.max(st, axis=0, keepdims=True))
            pt = jnp.exp2(st - m_new).astype(BF16)
            acc = jnp.exp2(m_i - m_new) * acc + jnp.dot(vt, pt, preferred_element_type=F32)
            state[p] = (m_new, acc)
    return tuple(state)


def _attn_init(tq):
    one = (jnp.full((1, 2 * tq), NEG, F32), jnp.zeros((V_ROWS, 2 * tq), F32))
    return (one,) * (GROUP // 2)


def _attn_finish(carry):
    return [acc[:HEAD_DIM] / acc[HEAD_DIM:HEAD_DIM + 1] for _, acc in carry]


def _split3(x):
    hi = x.astype(BF16)
    r1 = x - hi.astype(F32)
    mid = r1.astype(BF16)
    lo = (r1 - mid.astype(F32)).astype(BF16)
    return hi, mid, lo


def _cmp_branch(qt_ref, kc_ref, vct_ref, mt_ref, bias_ref, t0, tok):
    tq = qt_ref.shape[2]
    ncp = kc_ref.shape[2]
    nb = mt_ref.shape[0]
    kc = kc_ref[0, 0]
    vct = vct_ref[0, 0]
    blk_end = lax.broadcasted_iota(jnp.int32, (ncp, 2 * tq), 0) * CMP_STRIDE + (CMP_LEN - 1)
    valid = blk_end <= tok
    any_valid = tok >= CMP_LEN - 1
    psum = jnp.zeros((ncp, tq), F32)
    outs = []
    for q in _attn_queries(qt_ref):
        st = jnp.where(valid, jnp.dot(kc, q, preferred_element_type=F32), NEG)
        e = jnp.exp2(st - jnp.max(st, axis=0, keepdims=True))
        inv = jnp.where(any_valid, 1.0 / jnp.sum(e, axis=0, keepdims=True), 0.0)
        p = e * inv
        outs.append(jnp.dot(vct, p.astype(BF16), preferred_element_type=F32))
        psum = psum + (p[:, :tq] + p[:, tq:])

    mt = mt_ref[...]
    sc = jnp.zeros((nb, tq), F32)
    for part in _split3(psum):
        sc = sc + jnp.dot(mt, part, preferred_element_type=F32)
    jj = lax.broadcasted_iota(jnp.int32, (nb, tq), 0)
    cur = (t0 + lax.broadcasted_iota(jnp.int32, (nb, tq), 1)) // SLC_BLK
    forced = (jj == 0) | (jj == cur) | (jj == cur - 1)
    sc = jnp.where(forced, sc + FORCE_BONUS, sc)
    sc = jnp.where(jj <= cur, sc, NEG)

    ngrp = nb // SUBLANE

    def rank(sc, live):
        rows = [sc[g * SUBLANE:(g + 1) * SUBLANE] for g in range(live)]
        cnt = [jnp.zeros((SUBLANE, tq), jnp.int32) for _ in range(live)]
        sub = lax.broadcasted_iota(jnp.int32, (SUBLANE, tq), 0)
        for gp in range(live):
            for r in range(SUBLANE):
                other = jnp.broadcast_to(rows[gp][r:r + 1], (SUBLANE, tq))
                for g in range(live):
                    if g < gp:
                        before = other > rows[g]
                    elif g > gp:
                        before = other >= rows[g]
                    else:
                        before = (other > rows[g]) | ((other == rows[g]) & (sub > r))
                    cnt[g] = cnt[g] + before.astype(jnp.int32)
        groups = [jnp.where((cnt[g] < min(SLC_TOPK, nb)) & (rows[g] > NEG / 2), 0.0, NEG) for g in range(live)]
        if live < ngrp:
            groups.append(jnp.full(((ngrp - live) * SUBLANE, tq), NEG, F32))
        return jnp.concatenate(groups, axis=0)

    last_group = (t0 + tq - 1) // (SLC_BLK * SUBLANE)
    bias_ref[...] = lax.switch(jnp.minimum(last_group, ngrp - 1),
                               [functools.partial(rank, live=k + 1) for k in range(ngrp)], sc)
    return outs


def _overlap_t(seq):
    nc_pad = seq // CMP_STRIDE
    n_slc = seq // SLC_BLK
    c0 = np.arange(nc_pad)[None, :] * CMP_STRIDE
    j0 = np.arange(n_slc)[:, None] * SLC_BLK
    ov = np.clip(np.minimum(c0 + CMP_LEN, j0 + SLC_BLK) - np.maximum(c0, j0), 0, None) / CMP_LEN
    ov[:, nc_pad - 1] = 0.0
    return jnp.asarray(ov, dtype=BF16)


def _slc_pieces(qs, k_ref, vt_ref, bias_ref, tok, chunk, first, count, causal):
    tq = tok.shape[1] // 2
    bpc = SLC_CHUNK // SLC_BLK
    b8 = bias_ref[pl.ds(pl.multiple_of(chunk * bpc, bpc), bpc), :]
    b16 = jnp.concatenate([b8, jnp.zeros_like(b8)], axis=0).astype(BF16)
    brows = jnp.concatenate([b16, b16], axis=1)
    zrows = jnp.zeros((LANE - HEAD_DIM - 2 * bpc, 2 * tq), BF16)
    rhs = [jnp.concatenate([q, brows, zrows], axis=0) for q in qs]
    scores, vts, masks = [], [], []
    for j in range(count):
        start = pl.multiple_of(first + j * ATT_SUB, ATT_SUB)
        kc = k_ref[0, pl.ds(start, ATT_SUB), :]
        scores.append([jnp.dot(kc, r, preferred_element_type=F32) for r in rhs])
        vts.append(vt_ref[0, :, pl.ds(start, ATT_SUB)])
        key = start + lax.broadcasted_iota(jnp.int32, (ATT_SUB, 2 * tq), 0)
        masks.append(key <= tok if causal else None)
    return scores, vts, masks


def _slc_past(qs, k_ref, vt_ref, bias_ref, i, tok, carry):
    tq = tok.shape[1] // 2
    ck = SLC_CHUNK
    per = ck // ATT_SUB

    def chunks(c, carry, n):
        scores, vts, masks = [], [], []
        for u in range(n):
            cc = n * c + u if n > 1 else c
            s, v, m = _slc_pieces(qs, k_ref, vt_ref, bias_ref, tok, cc, cc * ck, per, False)
            scores, vts, masks = scores + s, vts + v, masks + m
        return _attn_pieces(scores, vts, masks, carry)

    whole = (i * tq) // ck
    carry = lax.fori_loop(0, whole // 2, functools.partial(chunks, n=2), carry)
    carry = lax.fori_loop(2 * (whole // 2), whole, functools.partial(chunks, n=1), carry)
    rest = (i * tq - whole * ck) // ATT_SUB
    assert ck % tq == 0 and ck // tq == 2

    def partial_chunk(carry):
        return _attn_pieces(*_slc_pieces(qs, k_ref, vt_ref, bias_ref, tok, whole, whole * ck,
                                         tq // ATT_SUB, False), carry)

    return lax.cond(rest > 0, partial_chunk, lambda carry: carry, carry)


def _win_pieces(qs, k_ref, vt_ref, n, tok):
    tq = tok.shape[1] // 2
    sub = ATT_SUB
    nback = WINDOW // tq
    scores, vts, masks = [], [], []
    for back in range(nback + 1):
        kb = n - back
        start = pl.multiple_of(jnp.maximum(kb, 0) * tq, tq)
        for j in range(tq // sub):
            kc = k_ref[0, pl.ds(start + j * sub, sub), :]
            scores.append([jnp.dot(kc, q, preferred_element_type=F32) for q in qs])
            vts.append(vt_ref[0, :, pl.ds(start + j * sub, sub)])
            key = start + j * sub + lax.broadcasted_iota(jnp.int32, (sub, 2 * tq), 0)
            if back == 0:
                masks.append(key <= tok)
            elif back == nback:
                masks.append((tok - key < WINDOW) & (kb >= 0))
            else:
                masks.append(jnp.broadcast_to(kb >= 0, (sub, 2 * tq)))
    return scores, vts, masks


def _nsa_kernel(qct_ref, qt_ref, gt_ref, kc_ref, vct_ref, mt_ref, ks_ref, vst_ref, kw_ref, vwt_ref,
                o_ref, bias_scr):
    tq = qt_ref.shape[2]
    i = pl.program_id(2)
    tok = i * tq + lax.broadcasted_iota(jnp.int32, (1, 2 * tq), 1) % tq
    qs = _attn_queries(qt_ref)
    o_cmp = _cmp_branch(qct_ref, kc_ref, vct_ref, mt_ref, bias_scr, i * tq, tok)
    own = _slc_pieces(qs, ks_ref, vst_ref, bias_scr, tok, (i * tq) // SLC_CHUNK, i * tq, tq // ATT_SUB, True)
    win = _win_pieces(qs, kw_ref, vwt_ref, i, tok)
    slc = _attn_pieces(*own, _attn_init(tq))
    o_win = _attn_finish(_attn_pieces(*win, _attn_init(tq)))
    o_slc = _attn_finish(_slc_past(qs, ks_ref, vst_ref, bias_scr, i, tok, slc))
    _attn_store(o_ref, gt_ref, (o_cmp, o_slc, o_win))


def _nsa(qct, qt, gt, kcmp, vcmpt, ks, vst, kw, vwt, batch, seq):
    tq = ATT_QTILE
    nq = seq // tq
    ncp = seq // CMP_STRIDE
    nb = seq // SLC_BLK
    qspec = pl.BlockSpec((GROUP, HEAD_DIM, tq), lambda b, k, i: (k, 0, b * nq + i))
    vspec = pl.BlockSpec((1, V_ROWS, seq), lambda b, k, i: (k, 0, b))
    return pl.pallas_call(
        _nsa_kernel,
        grid=(batch, N_KV, nq),
        in_specs=[qspec, qspec,
                  pl.BlockSpec((GROUP, SUBLANE, tq), lambda b, k, i: (k, 0, b * nq + i)),
                  pl.BlockSpec((1, 1, ncp, HEAD_DIM), lambda b, k, i: (b, k, 0, 0)),
                  pl.BlockSpec((1, 1, HEAD_DIM, ncp), lambda b, k, i: (b, k, 0, 0)),
                  pl.BlockSpec((nb, ncp), lambda b, k, i: (0, 0)),
                  pl.BlockSpec((1, seq, LANE), lambda b, k, i: (k, b, 0)), vspec,
                  pl.BlockSpec((1, seq, HEAD_DIM), lambda b, k, i: (k, b, 0)), vspec],
        out_specs=pl.BlockSpec((tq, GROUP * HEAD_DIM), lambda b, k, i: (b * nq + i, k)),
        out_shape=jax.ShapeDtypeStruct((batch * seq, D_NSA), F32),
        scratch_shapes=[pltpu.VMEM((nb, tq), F32)],
        compiler_params=_params(("parallel", "parallel", "parallel")),
        name="nsa",
    )(qct, qt, gt, kcmp, vcmpt, _overlap_t(seq), ks, vst, kw, vwt)


def _rope_tables(seq):
    inv = 1.0 / (ROPE_THETA ** (jnp.arange(0, HEAD_DIM, 2, dtype=F32) / HEAD_DIM))
    ang = jnp.arange(seq, dtype=F32)[:, None] * inv[None, :]
    cos, sin = jnp.cos(ang), jnp.sin(ang)
    reps = LANE // HEAD_DIM
    cosf = jnp.tile(jnp.concatenate([cos, cos], axis=-1), (1, reps))
    sinf = jnp.tile(jnp.concatenate([-sin, sin], axis=-1), (1, reps))
    return cosf, sinf


def _block_diag(w):
    h, n, _ = w.shape
    eye = jnp.eye(h, dtype=w.dtype)
    return (eye[:, None, :, None] * w[:, :, None, :]).reshape(h * n, h * n)


def _mixers(proj, batch, seq, conv_w, conv_b, wa, ba, wi, bi, lam, pos_k, pos_v, kw1, kw2, vw1, vw2):
    (u, gl, gn, gt, kcr, vcr, qct, qt, ks, vst, kwn, vwt) = proj
    wg = jnp.concatenate([_block_diag(wa), _block_diag(wi)], axis=1).astype(BF16)
    ylru = _lru(u, gl, conv_w, conv_b.reshape(1, D_LRU), wg, ba.reshape(1, D_LRU),
                bi.reshape(1, D_LRU), lam.reshape(1, D_LRU), batch, seq)
    ratio = CMP_LEN // CMP_STRIDE
    kcmp, vcmpt = _compress(
        kcr, vcr, pos_k.reshape(ratio, SUB_FLAT), pos_v.reshape(ratio, SUB_FLAT),
        kw1.reshape(ratio, SUB_FLAT, CMP_HID).astype(BF16), kw2.astype(BF16),
        vw1.reshape(ratio, SUB_FLAT, CMP_HID).astype(BF16), vw2.T.astype(BF16), batch, seq)
    return ylru, gn, _nsa(qct, qt, gt, kcmp, vcmpt, ks, vst, kwn, vwt, batch, seq)


def kernel(x, norm_g, w_in, conv_w, conv_b, lru_wa, lru_ba, lru_wi, lru_bi, lru_lambda, cmp_pos_k, cmp_pos_v, cmp_k_w1, cmp_k_w2, cmp_v_w1, cmp_v_w2, w_out, final_g):
    batch, seq, _ = x.shape
    assert seq % ROW_TILE == 0 and seq % SLC_CHUNK == 0 and (seq // SLC_BLK) % SUBLANE == 0
    cosf, sinf = _rope_tables(seq)
    depth = norm_g.shape[0]
    w_in_p = jnp.pad(w_in, ((0, 0), (0, 0), (0, D_IN_PAD - D_IN))).astype(BF16)
    w_out_b = w_out.astype(BF16)
    h = x.reshape(batch * seq, D_MODEL)
    prev = None
    for l in range(depth):
        proj = _inproj(h, norm_g[l].reshape(1, D_MODEL), w_in_p[l], cosf, sinf, seq, prev)
        if prev is not None:
            h, proj = proj[0], proj[1:]
        ylru, gn, yatt = _mixers(proj, batch, seq, conv_w[l], conv_b[l], lru_wa[l], lru_ba[l], lru_wi[l],
                                 lru_bi[l], lru_lambda[l], cmp_pos_k[l], cmp_pos_v[l], cmp_k_w1[l],
                                 cmp_k_w2[l], cmp_v_w1[l], cmp_v_w2[l])
        prev = (ylru, gn, yatt, w_out_b[l])
    out = _outnorm(h, *prev, final_g.reshape(1, D_MODEL))
    return out.reshape(batch, seq, D_MODEL)
```

```python
import functools

import numpy as np
import jax
import jax.numpy as jnp
from jax import lax
from jax.experimental import pallas as pl
from jax.experimental.pallas import tpu as pltpu

F32 = jnp.float32
BF16 = jnp.bfloat16

D_MODEL = 1024
D_LRU = 512
H_LRU = 8
LRU_BLK = D_LRU // H_LRU
CONV_W = 4
LRU_C = 8.0
D_NSA = 512
HEAD_DIM = 64
N_HEADS = 8
N_KV = 2
GROUP = N_HEADS // N_KV
KV_W = N_KV * HEAD_DIM
CMP_LEN = 32
CMP_STRIDE = 16
CMP_HID = 256
SLC_BLK = 64
SLC_TOPK = 16
WINDOW = 512
ROPE_THETA = 10000.0
EPS = 1e-6
NEG = -1e30
FORCE_BONUS = 1e4
SCALE = HEAD_DIM ** -0.5
LOG2E = 1.4426950408889634
D_IN = 2 * D_LRU + 2 * D_NSA + 6 * KV_W + 3 * N_HEADS
LANE = 128
SUBLANE = 8
D_IN_PAD = ((D_IN + LANE - 1) // LANE) * LANE
SUB_FLAT = CMP_STRIDE * HEAD_DIM
N_BRANCH = 3

OFF_U = 0
OFF_GL = D_LRU
OFF_Q = 2 * D_LRU
OFF_GN = 2 * D_LRU + D_NSA
OFF_KV = 2 * D_LRU + 2 * D_NSA

ROW_TILE = 512
LRU_TILE = 256
ATT_QTILE = 256
SLC_CHUNK = 512
ATT_SUB = 128
V_ROWS = 80
VMEM_LIMIT = 48 * 1024 * 1024

_NT = (((1,), (1,)), ((), ()))


def _params(sem):
    return pltpu.CompilerParams(dimension_semantics=sem, vmem_limit_bytes=VMEM_LIMIT)


def _rope(t, cos, sin_signed):
    lane = lax.broadcasted_iota(jnp.int32, t.shape, 1)
    first = (lane % HEAD_DIM) < (HEAD_DIM // 2)
    rot = jnp.where(first, pltpu.roll(t, LANE - HEAD_DIM // 2, 1), pltpu.roll(t, HEAD_DIM // 2, 1))
    return t * cos + rot * sin_signed


def _inproj_body(x, g_ref, w_ref, cos_ref, sin_ref,
                 u_ref, gl_ref, gn_ref, gt_ref, kcr_ref, vcr_ref,
                 qct_ref, qt_ref, ks_ref, vst_ref, kw_ref, vwt_ref, kc_scr, vc_scr):
    tm = x.shape[0]
    ms = jnp.mean(x * x, axis=-1, keepdims=True)
    h = ((x * lax.rsqrt(ms + EPS)) * g_ref[...]).astype(BF16)
    cos = cos_ref[...]
    sin = sin_ref[...]

    def proj(off, width):
        return jnp.dot(h, w_ref[:, off:off + width], preferred_element_type=F32)

    u_ref[...] = proj(OFF_U, D_LRU)
    gl_ref[...] = proj(OFF_GL, D_LRU)
    gn_ref[...] = proj(OFF_GN, D_NSA)
    q = proj(OFF_Q, D_NSA)
    kvg = proj(OFF_KV, D_IN_PAD - OFF_KV)
    group = lambda i: kvg[:, i * KV_W:(i + 1) * KV_W]

    def put_heads_t(ref, base, slab_t):
        ref[base, 0:HEAD_DIM, :] = slab_t[:HEAD_DIM].astype(ref.dtype)
        ref[base + 1, 0:HEAD_DIM, :] = slab_t[HEAD_DIM:].astype(ref.dtype)

    for i in range(D_NSA // LANE):
        slab = q[:, i * LANE:(i + 1) * LANE]
        put_heads_t(qct_ref, 2 * i, (slab * (SCALE * LOG2E)).T)
        put_heads_t(qt_ref, 2 * i, (_rope(slab, cos, sin) * (SCALE * LOG2E)).T)

    for scr, dst, i in ((kc_scr, kcr_ref, 0), (vc_scr, vcr_ref, 1)):
        scr[...] = group(i)
        for j in range(CMP_STRIDE):
            rows = scr[pl.ds(j, tm // CMP_STRIDE, stride=CMP_STRIDE), :]
            for hh in range(N_KV):
                dst[hh, :, j * HEAD_DIM:(j + 1) * HEAD_DIM] = rows[:, hh * HEAD_DIM:(hh + 1) * HEAD_DIM]

    ks = _rope(group(2), cos, sin)
    lane = lax.broadcasted_iota(jnp.int32, (tm, LANE), 1)
    blk = (lax.broadcasted_iota(jnp.int32, (tm, LANE), 0) // SLC_BLK) % (SLC_CHUNK // SLC_BLK)
    onehot = (lane - HEAD_DIM == blk).astype(F32)
    ks_ref[0] = jnp.where(lane < HEAD_DIM, ks, onehot).astype(ks_ref.dtype)
    ks_ref[1] = jnp.where(lane < HEAD_DIM, pltpu.roll(ks, HEAD_DIM, 1), onehot).astype(ks_ref.dtype)
    kw = _rope(group(4), cos, sin)
    kw_ref[0] = kw[:, :HEAD_DIM].astype(kw_ref.dtype)
    kw_ref[1] = kw[:, HEAD_DIM:].astype(kw_ref.dtype)

    tail = (lax.broadcasted_iota(jnp.int32, (V_ROWS - HEAD_DIM, tm), 0) == 0).astype(BF16)
    for ref, i in ((vst_ref, 3), (vwt_ref, 5)):
        put_heads_t(ref, 0, group(i).T)
        for hh in range(N_KV):
            ref[hh, HEAD_DIM:V_ROWS, :] = tail

    gate_t = group(6).T
    fill = jnp.zeros((SUBLANE - N_BRANCH, tm), F32)
    for hd in range(N_HEADS):
        gt_ref[hd] = jnp.concatenate([gate_t[N_BRANCH * hd:N_BRANCH * (hd + 1)], fill], axis=0)


def _outproj_rows(x_ref, y_ref, gn_ref, oa_ref, wo_ref):
    y_nsa = oa_ref[...] * jax.nn.silu(gn_ref[...])
    acc = jnp.dot(y_ref[...], wo_ref[0:D_LRU, :], preferred_element_type=F32)
    acc = acc + jnp.dot(y_nsa.astype(BF16), wo_ref[D_LRU:, :], preferred_element_type=F32)
    return x_ref[...] + acc


def _inproj_kernel(x_ref, *refs):
    _inproj_body(x_ref[...], *refs)


def _outin_kernel(x_ref, y_ref, gn_ref, oa_ref, wo_ref, g_ref, w_ref, cos_ref, sin_ref, xo_ref, *refs):
    x = _outproj_rows(x_ref, y_ref, gn_ref, oa_ref, wo_ref)
    xo_ref[...] = x
    _inproj_body(x, g_ref, w_ref, cos_ref, sin_ref, *refs)


def _outnorm_kernel(x_ref, y_ref, gn_ref, oa_ref, wo_ref, g_ref, o_ref):
    x = _outproj_rows(x_ref, y_ref, gn_ref, oa_ref, wo_ref)
    ms = jnp.mean(x * x, axis=-1, keepdims=True)
    o_ref[...] = (x * lax.rsqrt(ms + EPS)) * g_ref[...]


def _const(shape):
    return pl.BlockSpec(shape, lambda i: (0,) * len(shape), pipeline_mode=pl.Buffered(1))


def _row(width):
    return pl.BlockSpec((ROW_TILE, width), lambda i: (i, 0))


def _outproj_specs():
    return [_row(D_MODEL), _row(D_LRU), _row(D_NSA), _row(D_NSA), _const((D_MODEL, D_MODEL))]


def _outnorm(x2, ylru, gn, oatt, wo, g):
    return pl.pallas_call(
        _outnorm_kernel,
        grid=(x2.shape[0] // ROW_TILE,),
        in_specs=_outproj_specs() + [_const((1, D_MODEL))],
        out_specs=_row(D_MODEL),
        out_shape=jax.ShapeDtypeStruct(x2.shape, F32),
        compiler_params=_params(("parallel",)),
        name="outnorm",
    )(x2, ylru, gn, oatt, wo, g)


def _inproj(x2, g, w, cosf, sinf, seq, prev=None):
    m = x2.shape[0]
    tm = ROW_TILE
    spt = seq // tm
    assert tm % SLC_CHUNK == 0
    row = _row
    heads = lambda n, width: pl.BlockSpec((n, tm, width), lambda i: (0, i, 0))
    heads_t = lambda n, rows: pl.BlockSpec((n, rows, tm), lambda i: (0, 0, i))
    flat = pl.BlockSpec((N_KV, tm // CMP_STRIDE, SUB_FLAT), lambda i: (0, i, 0))
    tab = pl.BlockSpec((tm, LANE), lambda i: (i % spt, 0))
    out_shape = (
        jax.ShapeDtypeStruct((m, D_LRU), F32),
        jax.ShapeDtypeStruct((m, D_LRU), F32),
        jax.ShapeDtypeStruct((m, D_NSA), F32),
        jax.ShapeDtypeStruct((N_HEADS, SUBLANE, m), F32),
        jax.ShapeDtypeStruct((N_KV, m // CMP_STRIDE, SUB_FLAT), F32),
        jax.ShapeDtypeStruct((N_KV, m // CMP_STRIDE, SUB_FLAT), F32),
        jax.ShapeDtypeStruct((N_HEADS, HEAD_DIM, m), BF16),
        jax.ShapeDtypeStruct((N_HEADS, HEAD_DIM, m), BF16),
        jax.ShapeDtypeStruct((N_KV, m, LANE), BF16),
        jax.ShapeDtypeStruct((N_KV, V_ROWS, m), BF16),
        jax.ShapeDtypeStruct((N_KV, m, HEAD_DIM), BF16),
        jax.ShapeDtypeStruct((N_KV, V_ROWS, m), BF16),
    )
    out_specs = (row(D_LRU), row(D_LRU), row(D_NSA), heads_t(N_HEADS, SUBLANE), flat, flat,
                 heads_t(N_HEADS, HEAD_DIM), heads_t(N_HEADS, HEAD_DIM), heads(N_KV, LANE), heads_t(N_KV, V_ROWS),
                 heads(N_KV, HEAD_DIM), heads_t(N_KV, V_ROWS))
    proj_specs = [_const((1, D_MODEL)), _const((D_MODEL, D_IN_PAD)), tab, tab]
    scratch = [pltpu.VMEM((tm, KV_W), F32)] * 2
    if prev is None:
        return pl.pallas_call(
            _inproj_kernel,
            grid=(m // tm,),
            in_specs=[row(D_MODEL)] + proj_specs,
            out_specs=out_specs,
            out_shape=out_shape,
            scratch_shapes=scratch,
            compiler_params=_params(("parallel",)),
            name="inproj",
        )(x2, g, w, cosf, sinf)
    ylru, gn, oatt, wo = prev
    return pl.pallas_call(
        _outin_kernel,
        grid=(m // tm,),
        in_specs=_outproj_specs() + proj_specs,
        out_specs=(row(D_MODEL),) + out_specs,
        out_shape=(jax.ShapeDtypeStruct(x2.shape, F32),) + out_shape,
        scratch_shapes=scratch,
        compiler_params=_params(("parallel",)),
        name="outin",
    )(x2, ylru, gn, oatt, wo, g, w, cosf, sinf)


def _softplus(x):
    return jnp.maximum(x, 0.0) + jnp.log1p(jnp.exp(-jnp.abs(x)))


def _lru_kernel(u_ref, g_ref, cw_ref, cb_ref, wg_ref, ba_ref, bi_ref, lam_ref, y_ref,
                ubuf, hprev, hseq):
    t = u_ref.shape[0]
    head = SUBLANE

    @pl.when(pl.program_id(1) == 0)
    def _():
        ubuf[0:head, :] = jnp.zeros((head, D_LRU), F32)
        hprev[...] = jnp.zeros_like(hprev)

    ubuf[head:head + t, :] = u_ref[...]
    conv = cb_ref[...]
    for k in range(CONV_W):
        conv = conv + cw_ref[k:k + 1, :] * ubuf[pl.ds(head - (CONV_W - 1) + k, t), :]
    ubuf[head - (CONV_W - 1):head, :] = ubuf[head + t - (CONV_W - 1):head + t, :]

    pre = jnp.dot(conv.astype(BF16), wg_ref[...], preferred_element_type=F32)
    r = jax.nn.sigmoid(pre[:, :D_LRU] + ba_ref[...])
    gi = jax.nn.sigmoid(pre[:, D_LRU:] + bi_ref[...])
    log_a = (-LRU_C * r) * _softplus(-lam_ref[...])
    a = jnp.exp(log_a)
    drive = jnp.sqrt((1.0 - a) * (1.0 + a)) * (gi * conv)

    ng = t // SUBLANE
    a3 = a.reshape(ng, SUBLANE, D_LRU)
    b3 = drive.reshape(ng, SUBLANE, D_LRU)
    sub = lax.broadcasted_iota(jnp.int32, (ng, SUBLANE, D_LRU), 1)
    d = 1
    while d < SUBLANE:
        keep = sub >= d
        b3 = jnp.where(keep, b3 + a3 * pltpu.roll(b3, d, 1), b3)
        a3 = jnp.where(keep, a3 * pltpu.roll(a3, d, 1), a3)
        d *= 2
    h_in = jnp.broadcast_to(hprev[...], (SUBLANE, D_LRU))
    for gidx in range(ng):
        h_g = a3[gidx] * h_in + b3[gidx]
        hseq[gidx * SUBLANE:(gidx + 1) * SUBLANE, :] = h_g
        h_in = jnp.broadcast_to(h_g[SUBLANE - 1:SUBLANE, :], (SUBLANE, D_LRU))
    hprev[...] = h_in[0:1, :]
    y_ref[...] = (hseq[...] * jax.nn.silu(g_ref[...])).astype(y_ref.dtype)


def _lru(u, gl, cw, cb, wg, ba, bi, lam, batch, seq):
    t = LRU_TILE
    spt = seq // t
    row = pl.BlockSpec((t, D_LRU), lambda b, j: (b * spt + j, 0))
    vec = pl.BlockSpec((1, D_LRU), lambda b, j: (0, 0))
    return pl.pallas_call(
        _lru_kernel,
        grid=(batch, spt),
        in_specs=[row, row,
                  pl.BlockSpec((CONV_W, D_LRU), lambda b, j: (0, 0)), vec,
                  pl.BlockSpec((D_LRU, 2 * D_LRU), lambda b, j: (0, 0)), vec, vec, vec],
        out_specs=row,
        out_shape=jax.ShapeDtypeStruct(u.shape, BF16),
        scratch_shapes=[pltpu.VMEM((SUBLANE + t, D_LRU), F32), pltpu.VMEM((1, D_LRU), F32),
                        pltpu.VMEM((t, D_LRU), F32)],
        compiler_params=_params(("parallel", "arbitrary")),
        name="rglru",
    )(u, gl, cw, cb, wg, ba, bi, lam)


def _compress_kernel(k_ref, v_ref, pk_ref, pv_ref, k1_ref, k2_ref, v1_ref, v2t_ref, ko_ref, vo_ref, shift):
    n = k_ref.shape[1]

    def hidden(x_ref, p_ref, w1_ref):
        x = x_ref[0]
        lo = jnp.dot((x + p_ref[0:1, :]).astype(BF16), w1_ref[0], preferred_element_type=F32)
        hi = jnp.dot((x + p_ref[1:2, :]).astype(BF16), w1_ref[1], preferred_element_type=F32)
        shift[0:n, :] = hi
        shift[n:n + 8, :] = jnp.zeros((8, CMP_HID), F32)
        return jax.nn.silu(lo + shift[pl.ds(1, n), :]).astype(BF16)

    ko_ref[0, 0] = jnp.dot(hidden(k_ref, pk_ref, k1_ref), k2_ref[...],
                           preferred_element_type=F32).astype(ko_ref.dtype)
    vo_ref[0, 0] = lax.dot_general(v2t_ref[...], hidden(v_ref, pv_ref, v1_ref), _NT,
                                   preferred_element_type=F32).astype(vo_ref.dtype)


def _compress(kcr, vcr, pk, pv, k1, k2, v1, v2t, batch, seq):
    n = seq // CMP_STRIDE
    xin = pl.BlockSpec((1, n, SUB_FLAT), lambda b, k: (k, b, 0))
    pos = pl.BlockSpec((2, SUB_FLAT), lambda b, k: (0, 0))
    w1 = pl.BlockSpec((2, SUB_FLAT, CMP_HID), lambda b, k: (0, 0, 0))
    return pl.pallas_call(
        _compress_kernel,
        grid=(batch, N_KV),
        in_specs=[xin, xin, pos, pos, w1, pl.BlockSpec((CMP_HID, HEAD_DIM), lambda b, k: (0, 0)),
                  w1, pl.BlockSpec((HEAD_DIM, CMP_HID), lambda b, k: (0, 0))],
        out_specs=(pl.BlockSpec((1, 1, n, HEAD_DIM), lambda b, k: (b, k, 0, 0)),
                   pl.BlockSpec((1, 1, HEAD_DIM, n), lambda b, k: (b, k, 0, 0))),
        out_shape=(jax.ShapeDtypeStruct((batch, N_KV, n, HEAD_DIM), BF16),
                   jax.ShapeDtypeStruct((batch, N_KV, HEAD_DIM, n), BF16)),
        scratch_shapes=[pltpu.VMEM((n + 8, CMP_HID), F32)],
        compiler_params=_params(("parallel", "parallel")),
        name="compress",
    )(kcr, vcr, pk, pv, k1, k2, v1, v2t)


def _attn_queries(qt_ref):
    return [jnp.concatenate([qt_ref[2 * p], qt_ref[2 * p + 1]], axis=1) for p in range(GROUP // 2)]


def _attn_store(o_ref, gt_ref, branches):
    for p in range(GROUP // 2):
        tq = branches[0][p].shape[1] // 2
        halves = []
        for e in range(2):
            cols = slice(e * tq, (e + 1) * tq)
            terms = [jax.nn.sigmoid(gt_ref[2 * p + e, br:br + 1, :]) * outs[p][:, cols]
                     for br, outs in enumerate(branches)]
            halves.append((terms[0] + terms[1]) + terms[2])
        o_ref[:, p * 2 * HEAD_DIM:(p + 1) * 2 * HEAD_DIM] = jnp.concatenate(halves, axis=0).T


def _attn_pieces(scores, vts, masks, carry):
    state = list(carry)
    for j, vt in enumerate(vts):
        for p, (m_i, acc) in enumerate(state):
            st = scores[j][p]
            if masks[j] is not None:
                st = jnp.where(masks[j], st, NEG)
            m_new = jnp.maximum(m_i, jnp.max(st, axis=0, keepdims=True))
            pt = jnp.exp2(st - m_new).astype(BF16)
            acc = jnp.exp2(m_i - m_new) * acc + jnp.dot(vt, pt, preferred_element_type=F32)
            state[p] = (m_new, acc)
    return tuple(state)


def _attn_init(tq):
    one = (jnp.full((1, 2 * tq), NEG, F32), jnp.zeros((V_ROWS, 2 * tq), F32))
    return (one,) * (GROUP // 2)


def _attn_finish(carry):
    return [acc[:HEAD_DIM] / acc[HEAD_DIM:HEAD_DIM + 1] for _, acc in carry]


def _split3(x):
    hi = x.astype(BF16)
    r1 = x - hi.astype(F32)
    mid = r1.astype(BF16)
    lo = (r1 - mid.astype(F32)).astype(BF16)
    return hi, mid, lo


def _cmp_scores(qt_ref, kc_ref):
    kc = kc_ref[0, 0]
    return [jnp.dot(kc, q, preferred_element_type=F32) for q in _attn_queries(qt_ref)]


def _cmp_attend(scores, vct_ref, mt_ref, t0, tok):
    ncp, tq = scores[0].shape[0], scores[0].shape[1] // 2
    nb = mt_ref.shape[0]
    vct = vct_ref[0, 0]
    blk_end = lax.broadcasted_iota(jnp.int32, (ncp, 2 * tq), 0) * CMP_STRIDE + (CMP_LEN - 1)
    valid = blk_end <= tok
    any_valid = tok >= CMP_LEN - 1
    psum = jnp.zeros((ncp, tq), F32)
    outs = []
    for raw in scores:
        st = jnp.where(valid, raw, NEG)
        e = jnp.exp2(st - jnp.max(st, axis=0, keepdims=True))
        inv = jnp.where(any_valid, 1.0 / jnp.sum(e, axis=0, keepdims=True), 0.0)
        p = e * inv
        outs.append(jnp.dot(vct, p.astype(BF16), preferred_element_type=F32))
        psum = psum + (p[:, :tq] + p[:, tq:])

    mt = mt_ref[...]
    sc = jnp.zeros((nb, tq), F32)
    for part in _split3(psum):
        sc = sc + jnp.dot(mt, part, preferred_element_type=F32)
    jj = lax.broadcasted_iota(jnp.int32, (nb, tq), 0)
    cur = (t0 + lax.broadcasted_iota(jnp.int32, (nb, tq), 1)) // SLC_BLK
    forced = (jj == 0) | (jj == cur) | (jj == cur - 1)
    sc = jnp.where(forced, sc + FORCE_BONUS, sc)
    return outs, jnp.where(jj <= cur, sc, NEG)


def _select_blocks(sc, bias_ref, t0):
    nb, tq = sc.shape
    ngrp = nb // SUBLANE

    def rank(sc, live):
        rows = [sc[g * SUBLANE:(g + 1) * SUBLANE] for g in range(live)]
        cnt = [jnp.zeros((SUBLANE, tq), jnp.int32) for _ in range(live)]
        sub = lax.broadcasted_iota(jnp.int32, (SUBLANE, tq), 0)
        for gp in range(live):
            for r in range(SUBLANE):
                other = jnp.broadcast_to(rows[gp][r:r + 1], (SUBLANE, tq))
                for g in range(live):
                    if g < gp:
                        before = other > rows[g]
                    elif g > gp:
                        before = other >= rows[g]
                    else:
                        before = (other > rows[g]) | ((other == rows[g]) & (sub > r))
                    cnt[g] = cnt[g] + before.astype(jnp.int32)
        groups = [jnp.where((cnt[g] < min(SLC_TOPK, nb)) & (rows[g] > NEG / 2), 0.0, NEG) for g in range(live)]
        if live < ngrp:
            groups.append(jnp.full(((ngrp - live) * SUBLANE, tq), NEG, F32))
        return jnp.concatenate(groups, axis=0)

    last_group = (t0 + tq - 1) // (SLC_BLK * SUBLANE)
    bias_ref[...] = lax.switch(jnp.minimum(last_group, ngrp - 1),
                               [functools.partial(rank, live=k + 1) for k in range(ngrp)], sc)


def _overlap_t(seq):
    nc_pad = seq // CMP_STRIDE
    n_slc = seq // SLC_BLK
    c0 = np.arange(nc_pad)[None, :] * CMP_STRIDE
    j0 = np.arange(n_slc)[:, None] * SLC_BLK
    ov = np.clip(np.minimum(c0 + CMP_LEN, j0 + SLC_BLK) - np.maximum(c0, j0), 0, None) / CMP_LEN
    ov[:, nc_pad - 1] = 0.0
    return jnp.asarray(ov, dtype=BF16)


def _slc_pieces(qs, k_ref, vt_ref, bias_ref, tok, chunk, first, count, causal):
    tq = tok.shape[1] // 2
    bpc = SLC_CHUNK // SLC_BLK
    b8 = bias_ref[pl.ds(pl.multiple_of(chunk * bpc, bpc), bpc), :]
    b16 = jnp.concatenate([b8, jnp.zeros_like(b8)], axis=0).astype(BF16)
    brows = jnp.concatenate([b16, b16], axis=1)
    zrows = jnp.zeros((LANE - HEAD_DIM - 2 * bpc, 2 * tq), BF16)
    rhs = [jnp.concatenate([q, brows, zrows], axis=0) for q in qs]
    scores, vts, masks = [], [], []
    for j in range(count):
        start = pl.multiple_of(first + j * ATT_SUB, ATT_SUB)
        kc = k_ref[0, pl.ds(start, ATT_SUB), :]
        scores.append([jnp.dot(kc, r, preferred_element_type=F32) for r in rhs])
        vts.append(vt_ref[0, :, pl.ds(start, ATT_SUB)])
        row = lax.broadcasted_iota(jnp.int32, (ATT_SUB, 2 * tq), 0)
        masks.append(row <= tok - start if causal else None)
    return scores, vts, masks


def _slc_past(qs, k_ref, vt_ref, bias_ref, i, tok, carry):
    tq = tok.shape[1] // 2
    ck = SLC_CHUNK
    per = ck // ATT_SUB

    def chunks(c, carry, n):
        scores, vts, masks = [], [], []
        for u in range(n):
            cc = n * c + u if n > 1 else c
            s, v, m = _slc_pieces(qs, k_ref, vt_ref, bias_ref, tok, cc, cc * ck, per, False)
            scores, vts, masks = scores + s, vts + v, masks + m
        return _attn_pieces(scores, vts, masks, carry)

    whole = (i * tq) // ck
    carry = lax.fori_loop(0, whole // 2, functools.partial(chunks, n=2), carry)
    carry = lax.fori_loop(2 * (whole // 2), whole, functools.partial(chunks, n=1), carry)
    rest = (i * tq - whole * ck) // ATT_SUB
    assert ck % tq == 0 and ck // tq == 2

    def partial_chunk(carry):
        return _attn_pieces(*_slc_pieces(qs, k_ref, vt_ref, bias_ref, tok, whole, whole * ck,
                                         tq // ATT_SUB, False), carry)

    return lax.cond(rest > 0, partial_chunk, lambda carry: carry, carry)


def _win_pieces(qs, k_ref, vt_ref, n, tok):
    tq = tok.shape[1] // 2
    sub = ATT_SUB
    nback = WINDOW // tq
    scores, vts, masks = [], [], []
    for back in range(nback + 1):
        kb = n - back
        start = pl.multiple_of(jnp.maximum(kb, 0) * tq, tq)
        for j in range(tq // sub):
            kc = k_ref[0, pl.ds(start + j * sub, sub), :]
            scores.append([jnp.dot(kc, q, preferred_element_type=F32) for q in qs])
            vts.append(vt_ref[0, :, pl.ds(start + j * sub, sub)])
            row = lax.broadcasted_iota(jnp.int32, (sub, 2 * tq), 0)
            rel = tok - (start + j * sub)
            if back == 0:
                masks.append(row <= rel)
            elif back == nback:
                masks.append(row > jnp.where(kb >= 0, rel - WINDOW, sub))
            else:
                masks.append(jnp.broadcast_to(kb >= 0, (sub, 2 * tq)))
    return scores, vts, masks


def _nsa_kernel(qct_ref, qt_ref, gt_ref, kc_ref, vct_ref, mt_ref, ks_ref, vst_ref, kw_ref, vwt_ref,
                o_ref, bias_scr):
    tq = qt_ref.shape[2]
    i = pl.program_id(2)
    tok = i * tq + lax.broadcasted_iota(jnp.int32, (1, 2 * tq), 1) % tq
    qs = _attn_queries(qt_ref)
    cmp_scores = _cmp_scores(qct_ref, kc_ref)
    win = _win_pieces(qs, kw_ref, vwt_ref, i, tok)
    o_cmp, block_scores = _cmp_attend(cmp_scores, vct_ref, mt_ref, i * tq, tok)
    o_win = _attn_finish(_attn_pieces(*win, _attn_init(tq)))
    _select_blocks(block_scores, bias_scr, i * tq)
    own = _slc_pieces(qs, ks_ref, vst_ref, bias_scr, tok, (i * tq) // SLC_CHUNK, i * tq, tq // ATT_SUB, True)
    slc = _attn_pieces(*own, _attn_init(tq))
    o_slc = _attn_finish(_slc_past(qs, ks_ref, vst_ref, bias_scr, i, tok, slc))
    _attn_store(o_ref, gt_ref, (o_cmp, o_slc, o_win))


def _nsa(qct, qt, gt, kcmp, vcmpt, ks, vst, kw, vwt, batch, seq):
    tq = ATT_QTILE
    nq = seq // tq
    ncp = seq // CMP_STRIDE
    nb = seq // SLC_BLK
    qspec = pl.BlockSpec((GROUP, HEAD_DIM, tq), lambda b, k, i: (k, 0, b * nq + i))
    vspec = pl.BlockSpec((1, V_ROWS, seq), lambda b, k, i: (k, 0, b))
    return pl.pallas_call(
        _nsa_kernel,
        grid=(batch, N_KV, nq),
        in_specs=[qspec, qspec,
                  pl.BlockSpec((GROUP, SUBLANE, tq), lambda b, k, i: (k, 0, b * nq + i)),
                  pl.BlockSpec((1, 1, ncp, HEAD_DIM), lambda b, k, i: (b, k, 0, 0)),
                  pl.BlockSpec((1, 1, HEAD_DIM, ncp), lambda b, k, i: (b, k, 0, 0)),
                  pl.BlockSpec((nb, ncp), lambda b, k, i: (0, 0)),
                  pl.BlockSpec((1, seq, LANE), lambda b, k, i: (k, b, 0)), vspec,
                  pl.BlockSpec((1, seq, HEAD_DIM), lambda b, k, i: (k, b, 0)), vspec],
        out_specs=pl.BlockSpec((tq, GROUP * HEAD_DIM), lambda b, k, i: (b * nq + i, k)),
        out_shape=jax.ShapeDtypeStruct((batch * seq, D_NSA), F32),
        scratch_shapes=[pltpu.VMEM((nb, tq), F32)],
        compiler_params=_params(("parallel", "parallel", "parallel")),
        name="nsa",
    )(qct, qt, gt, kcmp, vcmpt, _overlap_t(seq), ks, vst, kw, vwt)


def _rope_tables(seq):
    inv = 1.0 / (ROPE_THETA ** (jnp.arange(0, HEAD_DIM, 2, dtype=F32) / HEAD_DIM))
    ang = jnp.arange(seq, dtype=F32)[:, None] * inv[None, :]
    cos, sin = jnp.cos(ang), jnp.sin(ang)
    reps = LANE // HEAD_DIM
    cosf = jnp.tile(jnp.concatenate([cos, cos], axis=-1), (1, reps))
    sinf = jnp.tile(jnp.concatenate([-sin, sin], axis=-1), (1, reps))
    return cosf, sinf


def _block_diag(w):
    h, n, _ = w.shape
    eye = jnp.eye(h, dtype=w.dtype)
    return (eye[:, None, :, None] * w[:, :, None, :]).reshape(h * n, h * n)


def _mixers(proj, batch, seq, conv_w, conv_b, wa, ba, wi, bi, lam, pos_k, pos_v, kw1, kw2, vw1, vw2):
    (u, gl, gn, gt, kcr, vcr, qct, qt, ks, vst, kwn, vwt) = proj
    wg = jnp.concatenate([_block_diag(wa), _block_diag(wi)], axis=1).astype(BF16)
    ylru = _lru(u, gl, conv_w, conv_b.reshape(1, D_LRU), wg, ba.reshape(1, D_LRU),
                bi.reshape(1, D_LRU), lam.reshape(1, D_LRU), batch, seq)
    ratio = CMP_LEN // CMP_STRIDE
    kcmp, vcmpt = _compress(
        kcr, vcr, pos_k.reshape(ratio, SUB_FLAT), pos_v.reshape(ratio, SUB_FLAT),
        kw1.reshape(ratio, SUB_FLAT, CMP_HID).astype(BF16), kw2.astype(BF16),
        vw1.reshape(ratio, SUB_FLAT, CMP_HID).astype(BF16), vw2.T.astype(BF16), batch, seq)
    return ylru, gn, _nsa(qct, qt, gt, kcmp, vcmpt, ks, vst, kwn, vwt, batch, seq)


def kernel(x, norm_g, w_in, conv_w, conv_b, lru_wa, lru_ba, lru_wi, lru_bi, lru_lambda, cmp_pos_k, cmp_pos_v, cmp_k_w1, cmp_k_w2, cmp_v_w1, cmp_v_w2, w_out, final_g):
    batch, seq, _ = x.shape
    assert seq % ROW_TILE == 0 and seq % SLC_CHUNK == 0 and (seq // SLC_BLK) % SUBLANE == 0
    cosf, sinf = _rope_tables(seq)
    depth = norm_g.shape[0]
    w_in_p = jnp.pad(w_in, ((0, 0), (0, 0), (0, D_IN_PAD - D_IN))).astype(BF16)
    w_out_b = w_out.astype(BF16)
    h = x.reshape(batch * seq, D_MODEL)
    prev = None
    for l in range(depth):
        proj = _inproj(h, norm_g[l].reshape(1, D_MODEL), w_in_p[l], cosf, sinf, seq, prev)
        if prev is not None:
            h, proj = proj[0], proj[1:]
        ylru, gn, yatt = _mixers(proj, batch, seq, conv_w[l], conv_b[l], lru_wa[l], lru_ba[l], lru_wi[l],
                                 lru_bi[l], lru_lambda[l], cmp_pos_k[l], cmp_pos_v[l], cmp_k_w1[l],
                                 cmp_k_w2[l], cmp_v_w1[l], cmp_v_w2[l])
        prev = (ylru, gn, yatt, w_out_b[l])
    out = _outnorm(h, *prev, final_g.reshape(1, D_MODEL))
    return out.reshape(batch, seq, D_MODEL)
```

```python
import functools

import numpy as np
import jax
import jax.numpy as jnp
from jax import lax
from jax.experimental import pallas as pl
from jax.experimental.pallas import tpu as pltpu

F32 = jnp.float32
BF16 = jnp.bfloat16

D_MODEL = 1024
D_LRU = 512
H_LRU = 8
LRU_BLK = D_LRU // H_LRU
CONV_W = 4
LRU_C = 8.0
D_NSA = 512
HEAD_DIM = 64
N_HEADS = 8
N_KV = 2
GROUP = N_HEADS // N_KV
KV_W = N_KV * HEAD_DIM
CMP_LEN = 32
CMP_STRIDE = 16
CMP_HID = 256
SLC_BLK = 64
SLC_TOPK = 16
WINDOW = 512
ROPE_THETA = 10000.0
EPS = 1e-6
NEG = -1e30
FORCE_BONUS = 1e4
SCALE = HEAD_DIM ** -0.5
LOG2E = 1.4426950408889634
D_IN = 2 * D_LRU + 2 * D_NSA + 6 * KV_W + 3 * N_HEADS
LANE = 128
SUBLANE = 8
D_IN_PAD = ((D_IN + LANE - 1) // LANE) * LANE
SUB_FLAT = CMP_STRIDE * HEAD_DIM
N_BRANCH = 3

OFF_U = 0
OFF_GL = D_LRU
OFF_Q = 2 * D_LRU
OFF_GN = 2 * D_LRU + D_NSA
OFF_KV = 2 * D_LRU + 2 * D_NSA

ROW_TILE = 512
LRU_TILE = 256
ATT_QTILE = 256
SLC_CHUNK = 512
ATT_SUB = 128
PAST_SUB = 256
V_ROWS = 80
VMEM_LIMIT = 48 * 1024 * 1024

_NT = (((1,), (1,)), ((), ()))


def _params(sem):
    return pltpu.CompilerParams(dimension_semantics=sem, vmem_limit_bytes=VMEM_LIMIT)


_sigmoid = jax.nn.sigmoid
_silu = jax.nn.silu


def _rope(t, cos, sin_signed):
    lane = lax.broadcasted_iota(jnp.int32, t.shape, 1)
    first = (lane % HEAD_DIM) < (HEAD_DIM // 2)
    rot = jnp.where(first, pltpu.roll(t, LANE - HEAD_DIM // 2, 1), pltpu.roll(t, HEAD_DIM // 2, 1))
    return t * cos + rot * sin_signed


def _inproj_body(x, g_ref, w_ref, cos_ref, sin_ref,
                 u_ref, gl_ref, gn_ref, gt_ref, kcr_ref, vcr_ref,
                 qct_ref, qt_ref, ks_ref, vst_ref, kw_ref, vwt_ref, kc_scr, vc_scr):
    tm = x.shape[0]
    ms = jnp.mean(x * x, axis=-1, keepdims=True)
    h = ((x * lax.rsqrt(ms + EPS)) * g_ref[...]).astype(BF16)
    cos = cos_ref[...]
    sin = sin_ref[...]

    def proj(off, width):
        return jnp.dot(h, w_ref[:, off:off + width], preferred_element_type=F32)

    u_ref[...] = proj(OFF_U, D_LRU)
    gl_ref[...] = proj(OFF_GL, D_LRU)
    gn_ref[...] = proj(OFF_GN, D_NSA)
    q = proj(OFF_Q, D_NSA)
    kvg = proj(OFF_KV, D_IN_PAD - OFF_KV)
    group = lambda i: kvg[:, i * KV_W:(i + 1) * KV_W]

    def put_heads_t(ref, base, slab_t):
        ref[base, 0:HEAD_DIM, :] = slab_t[:HEAD_DIM].astype(ref.dtype)
        ref[base + 1, 0:HEAD_DIM, :] = slab_t[HEAD_DIM:].astype(ref.dtype)

    for i in range(D_NSA // LANE):
        slab = q[:, i * LANE:(i + 1) * LANE]
        put_heads_t(qct_ref, 2 * i, (slab * (SCALE * LOG2E)).T)
        put_heads_t(qt_ref, 2 * i, (_rope(slab, cos, sin) * (SCALE * LOG2E)).T)

    for scr, dst, i in ((kc_scr, kcr_ref, 0), (vc_scr, vcr_ref, 1)):
        scr[...] = group(i)
        for j in range(CMP_STRIDE):
            rows = scr[pl.ds(j, tm // CMP_STRIDE, stride=CMP_STRIDE), :]
            for hh in range(N_KV):
                dst[hh, :, j * HEAD_DIM:(j + 1) * HEAD_DIM] = rows[:, hh * HEAD_DIM:(hh + 1) * HEAD_DIM]

    ks = _rope(group(2), cos, sin)
    lane = lax.broadcasted_iota(jnp.int32, (tm, LANE), 1)
    blk = (lax.broadcasted_iota(jnp.int32, (tm, LANE), 0) // SLC_BLK) % (SLC_CHUNK // SLC_BLK)
    onehot = (lane - HEAD_DIM == blk).astype(F32)
    ks_ref[0] = jnp.where(lane < HEAD_DIM, ks, onehot).astype(ks_ref.dtype)
    ks_ref[1] = jnp.where(lane < HEAD_DIM, pltpu.roll(ks, HEAD_DIM, 1), onehot).astype(ks_ref.dtype)
    kw = _rope(group(4), cos, sin)
    kw_ref[0] = kw[:, :HEAD_DIM].astype(kw_ref.dtype)
    kw_ref[1] = kw[:, HEAD_DIM:].astype(kw_ref.dtype)

    tail = (lax.broadcasted_iota(jnp.int32, (V_ROWS - HEAD_DIM, tm), 0) == 0).astype(BF16)
    for ref, i in ((vst_ref, 3), (vwt_ref, 5)):
        put_heads_t(ref, 0, group(i).T)
        for hh in range(N_KV):
            ref[hh, HEAD_DIM:V_ROWS, :] = tail

    gate_t = group(6).T
    fill = jnp.zeros((SUBLANE - N_BRANCH, tm), F32)
    for hd in range(N_HEADS):
        gt_ref[hd] = jnp.concatenate([gate_t[N_BRANCH * hd:N_BRANCH * (hd + 1)], fill], axis=0)


def _outproj_rows(x_ref, y_ref, gn_ref, oa_ref, wo_ref):
    y_nsa = oa_ref[...] * _silu(gn_ref[...])
    acc = jnp.dot(y_ref[...], wo_ref[0:D_LRU, :], preferred_element_type=F32)
    acc = acc + jnp.dot(y_nsa.astype(BF16), wo_ref[D_LRU:, :], preferred_element_type=F32)
    return x_ref[...] + acc


def _inproj_kernel(x_ref, *refs):
    _inproj_body(x_ref[...], *refs)


def _outin_kernel(x_ref, y_ref, gn_ref, oa_ref, wo_ref, g_ref, w_ref, cos_ref, sin_ref, xo_ref, *refs):
    x = _outproj_rows(x_ref, y_ref, gn_ref, oa_ref, wo_ref)
    xo_ref[...] = x
    _inproj_body(x, g_ref, w_ref, cos_ref, sin_ref, *refs)


def _outnorm_kernel(x_ref, y_ref, gn_ref, oa_ref, wo_ref, g_ref, o_ref):
    x = _outproj_rows(x_ref, y_ref, gn_ref, oa_ref, wo_ref)
    ms = jnp.mean(x * x, axis=-1, keepdims=True)
    o_ref[...] = (x * lax.rsqrt(ms + EPS)) * g_ref[...]


def _const(shape):
    return pl.BlockSpec(shape, lambda i: (0,) * len(shape), pipeline_mode=pl.Buffered(1))


def _row(width):
    return pl.BlockSpec((ROW_TILE, width), lambda i: (i, 0))


def _outproj_specs():
    return [_row(D_MODEL), _row(D_LRU), _row(D_NSA), _row(D_NSA), _const((D_MODEL, D_MODEL))]


def _outnorm(x2, ylru, gn, oatt, wo, g):
    return pl.pallas_call(
        _outnorm_kernel,
        grid=(x2.shape[0] // ROW_TILE,),
        in_specs=_outproj_specs() + [_const((1, D_MODEL))],
        out_specs=_row(D_MODEL),
        out_shape=jax.ShapeDtypeStruct(x2.shape, F32),
        compiler_params=_params(("parallel",)),
        name="outnorm",
    )(x2, ylru, gn, oatt, wo, g)


def _inproj(x2, g, w, cosf, sinf, seq, prev=None):
    m = x2.shape[0]
    tm = ROW_TILE
    spt = seq // tm
    assert tm % SLC_CHUNK == 0
    row = _row
    heads = lambda n, width: pl.BlockSpec((n, tm, width), lambda i: (0, i, 0))
    heads_t = lambda n, rows: pl.BlockSpec((n, rows, tm), lambda i: (0, 0, i))
    flat = pl.BlockSpec((N_KV, tm // CMP_STRIDE, SUB_FLAT), lambda i: (0, i, 0))
    tab = pl.BlockSpec((tm, LANE), lambda i: (i % spt, 0))
    out_shape = (
        jax.ShapeDtypeStruct((m, D_LRU), F32),
        jax.ShapeDtypeStruct((m, D_LRU), F32),
        jax.ShapeDtypeStruct((m, D_NSA), F32),
        jax.ShapeDtypeStruct((N_HEADS, SUBLANE, m), F32),
        jax.ShapeDtypeStruct((N_KV, m // CMP_STRIDE, SUB_FLAT), F32),
        jax.ShapeDtypeStruct((N_KV, m // CMP_STRIDE, SUB_FLAT), F32),
        jax.ShapeDtypeStruct((N_HEADS, HEAD_DIM, m), BF16),
        jax.ShapeDtypeStruct((N_HEADS, HEAD_DIM, m), BF16),
        jax.ShapeDtypeStruct((N_KV, m, LANE), BF16),
        jax.ShapeDtypeStruct((N_KV, V_ROWS, m), BF16),
        jax.ShapeDtypeStruct((N_KV, m, HEAD_DIM), BF16),
        jax.ShapeDtypeStruct((N_KV, V_ROWS, m), BF16),
    )
    out_specs = (row(D_LRU), row(D_LRU), row(D_NSA), heads_t(N_HEADS, SUBLANE), flat, flat,
                 heads_t(N_HEADS, HEAD_DIM), heads_t(N_HEADS, HEAD_DIM), heads(N_KV, LANE), heads_t(N_KV, V_ROWS),
                 heads(N_KV, HEAD_DIM), heads_t(N_KV, V_ROWS))
    proj_specs = [_const((1, D_MODEL)), _const((D_MODEL, D_IN_PAD)), tab, tab]
    scratch = [pltpu.VMEM((tm, KV_W), F32)] * 2
    if prev is None:
        return pl.pallas_call(
            _inproj_kernel,
            grid=(m // tm,),
            in_specs=[row(D_MODEL)] + proj_specs,
            out_specs=out_specs,
            out_shape=out_shape,
            scratch_shapes=scratch,
            compiler_params=_params(("parallel",)),
            name="inproj",
        )(x2, g, w, cosf, sinf)
    ylru, gn, oatt, wo = prev
    return pl.pallas_call(
        _outin_kernel,
        grid=(m // tm,),
        in_specs=_outproj_specs() + proj_specs,
        out_specs=(row(D_MODEL),) + out_specs,
        out_shape=(jax.ShapeDtypeStruct(x2.shape, F32),) + out_shape,
        scratch_shapes=scratch,
        compiler_params=_params(("parallel",)),
        name="outin",
    )(x2, ylru, gn, oatt, wo, g, w, cosf, sinf)


def _softplus(x):
    return jnp.maximum(x, 0.0) + jnp.log1p(jnp.exp(-jnp.abs(x)))


def _lru_kernel(u_ref, g_ref, cw_ref, cb_ref, wg_ref, ba_ref, bi_ref, lam_ref, y_ref,
                ubuf, hprev, hseq):
    t = u_ref.shape[0]
    head = SUBLANE

    @pl.when(pl.program_id(1) == 0)
    def _():
        ubuf[0:head, :] = jnp.zeros((head, D_LRU), F32)
        hprev[...] = jnp.zeros_like(hprev)

    ubuf[head:head + t, :] = u_ref[...]
    conv = cb_ref[...]
    for k in range(CONV_W):
        conv = conv + cw_ref[k:k + 1, :] * ubuf[pl.ds(head - (CONV_W - 1) + k, t), :]
    ubuf[head - (CONV_W - 1):head, :] = ubuf[head + t - (CONV_W - 1):head + t, :]

    pre = jnp.dot(conv.astype(BF16), wg_ref[...], preferred_element_type=F32)
    r = _sigmoid(pre[:, :D_LRU] + ba_ref[...])
    gi = _sigmoid(pre[:, D_LRU:] + bi_ref[...])
    log_a = (-LRU_C * r) * _softplus(-lam_ref[...])
    a = jnp.exp(log_a)
    drive = jnp.sqrt((1.0 - a) * (1.0 + a)) * (gi * conv)

    ng = t // SUBLANE
    a3 = a.reshape(ng, SUBLANE, D_LRU)
    b3 = drive.reshape(ng, SUBLANE, D_LRU)
    sub = lax.broadcasted_iota(jnp.int32, (ng, SUBLANE, D_LRU), 1)
    d = 1
    while d < SUBLANE:
        keep = sub >= d
        b3 = jnp.where(keep, b3 + a3 * pltpu.roll(b3, d, 1), b3)
        a3 = jnp.where(keep, a3 * pltpu.roll(a3, d, 1), a3)
        d *= 2
    h_in = jnp.broadcast_to(hprev[...], (SUBLANE, D_LRU))
    for gidx in range(ng):
        h_g = a3[gidx] * h_in + b3[gidx]
        hseq[gidx * SUBLANE:(gidx + 1) * SUBLANE, :] = h_g
        h_in = jnp.broadcast_to(h_g[SUBLANE - 1:SUBLANE, :], (SUBLANE, D_LRU))
    hprev[...] = h_in[0:1, :]
    y_ref[...] = (hseq[...] * _silu(g_ref[...])).astype(y_ref.dtype)


def _lru(u, gl, cw, cb, wg, ba, bi, lam, batch, seq):
    t = LRU_TILE
    spt = seq // t
    row = pl.BlockSpec((t, D_LRU), lambda b, j: (b * spt + j, 0))
    vec = pl.BlockSpec((1, D_LRU), lambda b, j: (0, 0))
    return pl.pallas_call(
        _lru_kernel,
        grid=(batch, spt),
        in_specs=[row, row,
                  pl.BlockSpec((CONV_W, D_LRU), lambda b, j: (0, 0)), vec,
                  pl.BlockSpec((D_LRU, 2 * D_LRU), lambda b, j: (0, 0)), vec, vec, vec],
        out_specs=row,
        out_shape=jax.ShapeDtypeStruct(u.shape, BF16),
        scratch_shapes=[pltpu.VMEM((SUBLANE + t, D_LRU), F32), pltpu.VMEM((1, D_LRU), F32),
                        pltpu.VMEM((t, D_LRU), F32)],
        compiler_params=_params(("parallel", "arbitrary")),
        name="rglru",
    )(u, gl, cw, cb, wg, ba, bi, lam)


def _compress_kernel(k_ref, v_ref, pk_ref, pv_ref, k1_ref, k2_ref, v1_ref, v2t_ref, ko_ref, vo_ref, shift):
    n = k_ref.shape[1]

    def hidden(x_ref, p_ref, w1_ref):
        x = x_ref[0]
        lo = jnp.dot((x + p_ref[0:1, :]).astype(BF16), w1_ref[0], preferred_element_type=F32)
        hi = jnp.dot((x + p_ref[1:2, :]).astype(BF16), w1_ref[1], preferred_element_type=F32)
        shift[0:n, :] = hi
        shift[n:n + 8, :] = jnp.zeros((8, CMP_HID), F32)
        return _silu(lo + shift[pl.ds(1, n), :]).astype(BF16)

    ko_ref[0, 0] = jnp.dot(hidden(k_ref, pk_ref, k1_ref), k2_ref[...],
                           preferred_element_type=F32).astype(ko_ref.dtype)
    vo_ref[0, 0] = lax.dot_general(v2t_ref[...], hidden(v_ref, pv_ref, v1_ref), _NT,
                                   preferred_element_type=F32).astype(vo_ref.dtype)


def _compress(kcr, vcr, pk, pv, k1, k2, v1, v2t, batch, seq):
    n = seq // CMP_STRIDE
    xin = pl.BlockSpec((1, n, SUB_FLAT), lambda b, k: (k, b, 0))
    pos = pl.BlockSpec((2, SUB_FLAT), lambda b, k: (0, 0))
    w1 = pl.BlockSpec((2, SUB_FLAT, CMP_HID), lambda b, k: (0, 0, 0))
    return pl.pallas_call(
        _compress_kernel,
        grid=(batch, N_KV),
        in_specs=[xin, xin, pos, pos, w1, pl.BlockSpec((CMP_HID, HEAD_DIM), lambda b, k: (0, 0)),
                  w1, pl.BlockSpec((HEAD_DIM, CMP_HID), lambda b, k: (0, 0))],
        out_specs=(pl.BlockSpec((1, 1, n, HEAD_DIM), lambda b, k: (b, k, 0, 0)),
                   pl.BlockSpec((1, 1, HEAD_DIM, n), lambda b, k: (b, k, 0, 0))),
        out_shape=(jax.ShapeDtypeStruct((batch, N_KV, n, HEAD_DIM), BF16),
                   jax.ShapeDtypeStruct((batch, N_KV, HEAD_DIM, n), BF16)),
        scratch_shapes=[pltpu.VMEM((n + 8, CMP_HID), F32)],
        compiler_params=_params(("parallel", "parallel")),
        name="compress",
    )(kcr, vcr, pk, pv, k1, k2, v1, v2t)


def _attn_queries(qt_ref):
    return [jnp.concatenate([qt_ref[2 * p], qt_ref[2 * p + 1]], axis=1) for p in range(GROUP // 2)]


def _attn_store(o_ref, gt_ref, branches):
    for p in range(GROUP // 2):
        tq = branches[0][p].shape[1] // 2
        halves = []
        for e in range(2):
            cols = slice(e * tq, (e + 1) * tq)
            terms = [_sigmoid(gt_ref[2 * p + e, br:br + 1, :]) * outs[p][:, cols]
                     for br, outs in enumerate(branches)]
            halves.append((terms[0] + terms[1]) + terms[2])
        o_ref[:, p * 2 * HEAD_DIM:(p + 1) * 2 * HEAD_DIM] = jnp.concatenate(halves, axis=0).T


def _attn_pieces(scores, vts, masks, carry):
    state = list(carry)
    for j, vt in enumerate(vts):
        for p, (m_i, acc) in enumerate(state):
            st = scores[j][p]
            if masks[j] is not None:
                st = jnp.where(masks[j], st, NEG)
            m_new = jnp.maximum(m_i, jnp.max(st, axis=0, keepdims=True))
            pt = jnp.exp2(st - m_new).astype(BF16)
            acc = jnp.exp2(m_i - m_new) * acc + jnp.dot(vt, pt, preferred_element_type=F32)
            state[p] = (m_new, acc)
    return tuple(state)


def _attn_init(tq):
    one = (jnp.full((1, 2 * tq), NEG, F32), jnp.zeros((V_ROWS, 2 * tq), F32))
    return (one,) * (GROUP // 2)


def _attn_finish(carry):
    return [acc[:HEAD_DIM] / acc[HEAD_DIM:HEAD_DIM + 1] for _, acc in carry]


def _split3(x):
    hi = x.astype(BF16)
    r1 = x - hi.astype(F32)
    mid = r1.astype(BF16)
    lo = (r1 - mid.astype(F32)).astype(BF16)
    return hi, mid, lo


def _cmp_scores(qt_ref, kc_ref):
    kc = kc_ref[0, 0]
    return [jnp.dot(kc, q, preferred_element_type=F32) for q in _attn_queries(qt_ref)]


def _cmp_attend(scores, vct_ref, mt_ref, t0, tok):
    ncp, tq = scores[0].shape[0], scores[0].shape[1] // 2
    nb = mt_ref.shape[0]
    vct = vct_ref[0, 0]
    blk_end = lax.broadcasted_iota(jnp.int32, (ncp, 2 * tq), 0) * CMP_STRIDE + (CMP_LEN - 1)
    valid = blk_end <= tok
    any_valid = tok >= CMP_LEN - 1
    psum = jnp.zeros((ncp, tq), F32)
    outs = []
    for raw in scores:
        st = jnp.where(valid, raw, NEG)
        e = jnp.exp2(st - jnp.max(st, axis=0, keepdims=True))
        inv = jnp.where(any_valid, 1.0 / jnp.sum(e, axis=0, keepdims=True), 0.0)
        p = e * inv
        outs.append(jnp.dot(vct, p.astype(BF16), preferred_element_type=F32))
        psum = psum + (p[:, :tq] + p[:, tq:])

    mt = mt_ref[...]
    sc = jnp.zeros((nb, tq), F32)
    for part in _split3(psum):
        sc = sc + jnp.dot(mt, part, preferred_element_type=F32)
    jj = lax.broadcasted_iota(jnp.int32, (nb, tq), 0)
    cur = (t0 + lax.broadcasted_iota(jnp.int32, (nb, tq), 1)) // SLC_BLK
    forced = (jj == 0) | (jj == cur) | (jj == cur - 1)
    sc = jnp.where(forced, sc + FORCE_BONUS, sc)
    return outs, jnp.where(jj <= cur, sc, NEG)


def _select_blocks(sc, bias_ref, t0):
    nb, tq = sc.shape
    ngrp = nb // SUBLANE

    def rank(sc, live):
        rows = [sc[g * SUBLANE:(g + 1) * SUBLANE] for g in range(live)]
        cnt = [jnp.zeros((SUBLANE, tq), jnp.int32) for _ in range(live)]
        sub = lax.broadcasted_iota(jnp.int32, (SUBLANE, tq), 0)
        for gp in range(live):
            for r in range(SUBLANE):
                other = jnp.broadcast_to(rows[gp][r:r + 1], (SUBLANE, tq))
                for g in range(live):
                    if g < gp:
                        before = other > rows[g]
                    elif g > gp:
                        before = other >= rows[g]
                    else:
                        before = (other > rows[g]) | ((other == rows[g]) & (sub > r))
                    cnt[g] = cnt[g] + before.astype(jnp.int32)
        groups = [jnp.where((cnt[g] < min(SLC_TOPK, nb)) & (rows[g] > NEG / 2), 0.0, NEG) for g in range(live)]
        if live < ngrp:
            groups.append(jnp.full(((ngrp - live) * SUBLANE, tq), NEG, F32))
        return jnp.concatenate(groups, axis=0)

    last_group = (t0 + tq - 1) // (SLC_BLK * SUBLANE)
    bias_ref[...] = lax.switch(jnp.minimum(last_group, ngrp - 1),
                               [functools.partial(rank, live=k + 1) for k in range(ngrp)], sc)


def _overlap_t(seq):
    nc_pad = seq // CMP_STRIDE
    n_slc = seq // SLC_BLK
    c0 = np.arange(nc_pad)[None, :] * CMP_STRIDE
    j0 = np.arange(n_slc)[:, None] * SLC_BLK
    ov = np.clip(np.minimum(c0 + CMP_LEN, j0 + SLC_BLK) - np.maximum(c0, j0), 0, None) / CMP_LEN
    ov[:, nc_pad - 1] = 0.0
    return jnp.asarray(ov, dtype=BF16)


def _slc_pieces(qs, k_ref, vt_ref, bias_ref, tok, chunk, first, count, causal, size=ATT_SUB):
    tq = tok.shape[1] // 2
    bpc = SLC_CHUNK // SLC_BLK
    b8 = bias_ref[pl.ds(pl.multiple_of(chunk * bpc, bpc), bpc), :]
    b16 = jnp.concatenate([b8, jnp.zeros_like(b8)], axis=0).astype(BF16)
    brows = jnp.concatenate([b16, b16], axis=1)
    zrows = jnp.zeros((LANE - HEAD_DIM - 2 * bpc, 2 * tq), BF16)
    rhs = [jnp.concatenate([q, brows, zrows], axis=0) for q in qs]
    scores, vts, masks = [], [], []
    for j in range(count):
        start = pl.multiple_of(first + j * size, size)
        kc = k_ref[0, pl.ds(start, size), :]
        scores.append([jnp.dot(kc, r, preferred_element_type=F32) for r in rhs])
        vts.append(vt_ref[0, :, pl.ds(start, size)])
        row = lax.broadcasted_iota(jnp.int32, (size, 2 * tq), 0)
        masks.append(row <= tok - start if causal else None)
    return scores, vts, masks


def _slc_recent(qs, k_ref, vt_ref, bias_ref, i, tok):
    tq = tok.shape[1] // 2
    ck = SLC_CHUNK
    assert ck % tq == 0 and ck // tq == 2
    chunk = (i * tq) // ck
    own = functools.partial(_slc_pieces, qs, k_ref, vt_ref, bias_ref, tok, chunk, i * tq, tq // ATT_SUB, True)

    def own_only():
        return _attn_pieces(*own(), _attn_init(tq))

    def own_and_earlier():
        earlier = _slc_pieces(qs, k_ref, vt_ref, bias_ref, tok, chunk, chunk * ck, tq // ATT_SUB, False)
        return _attn_pieces(*[a + b for a, b in zip(own(), earlier)], _attn_init(tq))

    return lax.cond(i * tq > chunk * ck, own_and_earlier, own_only)


def _slc_past(qs, k_ref, vt_ref, bias_ref, i, tok, carry):
    tq = tok.shape[1] // 2
    ck = SLC_CHUNK
    per = ck // PAST_SUB

    def chunks(c, carry, n):
        scores, vts, masks = [], [], []
        for u in range(n):
            cc = n * c + u if n > 1 else c
            s, v, m = _slc_pieces(qs, k_ref, vt_ref, bias_ref, tok, cc, cc * ck, per, False, PAST_SUB)
            scores, vts, masks = scores + s, vts + v, masks + m
        return _attn_pieces(scores, vts, masks, carry)

    whole = (i * tq) // ck
    carry = lax.fori_loop(0, whole // 2, functools.partial(chunks, n=2), carry)
    return lax.fori_loop(2 * (whole // 2), whole, functools.partial(chunks, n=1), carry)


def _win_pieces(qs, k_ref, vt_ref, n, tok):
    tq = tok.shape[1] // 2
    sub = ATT_SUB
    nback = WINDOW // tq
    scores, vts, masks = [], [], []
    for back in range(nback + 1):
        kb = n - back
        start = pl.multiple_of(jnp.maximum(kb, 0) * tq, tq)
        for j in range(tq // sub):
            kc = k_ref[0, pl.ds(start + j * sub, sub), :]
            scores.append([jnp.dot(kc, q, preferred_element_type=F32) for q in qs])
            vts.append(vt_ref[0, :, pl.ds(start + j * sub, sub)])
            row = lax.broadcasted_iota(jnp.int32, (sub, 2 * tq), 0)
            rel = tok - (start + j * sub)
            if back == 0:
                masks.append(row <= rel)
            elif back == nback:
                masks.append(row > jnp.where(kb >= 0, rel - WINDOW, sub))
            else:
                masks.append(jnp.broadcast_to(kb >= 0, (sub, 2 * tq)))
    return scores, vts, masks


def _nsa_kernel(qct_ref, qt_ref, gt_ref, kc_ref, vct_ref, mt_ref, ks_ref, vst_ref, kw_ref, vwt_ref,
                o_ref, bias_scr):
    tq = qt_ref.shape[2]
    i = pl.program_id(2)
    tok = i * tq + lax.broadcasted_iota(jnp.int32, (1, 2 * tq), 1) % tq
    qs = _attn_queries(qt_ref)
    cmp_scores = _cmp_scores(qct_ref, kc_ref)
    win = _win_pieces(qs, kw_ref, vwt_ref, i, tok)
    o_cmp, block_scores = _cmp_attend(cmp_scores, vct_ref, mt_ref, i * tq, tok)
    o_win = _attn_finish(_attn_pieces(*win, _attn_init(tq)))
    _select_blocks(block_scores, bias_scr, i * tq)
    slc = _slc_recent(qs, ks_ref, vst_ref, bias_scr, i, tok)
    o_slc = _attn_finish(_slc_past(qs, ks_ref, vst_ref, bias_scr, i, tok, slc))
    _attn_store(o_ref, gt_ref, (o_cmp, o_slc, o_win))


def _nsa(qct, qt, gt, kcmp, vcmpt, ks, vst, kw, vwt, batch, seq):
    tq = ATT_QTILE
    nq = seq // tq
    ncp = seq // CMP_STRIDE
    nb = seq // SLC_BLK
    qspec = pl.BlockSpec((GROUP, HEAD_DIM, tq), lambda b, k, i: (k, 0, b * nq + i))
    vspec = pl.BlockSpec((1, V_ROWS, seq), lambda b, k, i: (k, 0, b))
    return pl.pallas_call(
        _nsa_kernel,
        grid=(batch, N_KV, nq),
        in_specs=[qspec, qspec,
                  pl.BlockSpec((GROUP, SUBLANE, tq), lambda b, k, i: (k, 0, b * nq + i)),
                  pl.BlockSpec((1, 1, ncp, HEAD_DIM), lambda b, k, i: (b, k, 0, 0)),
                  pl.BlockSpec((1, 1, HEAD_DIM, ncp), lambda b, k, i: (b, k, 0, 0)),
                  pl.BlockSpec((nb, ncp), lambda b, k, i: (0, 0)),
                  pl.BlockSpec((1, seq, LANE), lambda b, k, i: (k, b, 0)), vspec,
                  pl.BlockSpec((1, seq, HEAD_DIM), lambda b, k, i: (k, b, 0)), vspec],
        out_specs=pl.BlockSpec((tq, GROUP * HEAD_DIM), lambda b, k, i: (b * nq + i, k)),
        out_shape=jax.ShapeDtypeStruct((batch * seq, D_NSA), F32),
        scratch_shapes=[pltpu.VMEM((nb, tq), F32)],
        compiler_params=_params(("parallel", "parallel", "parallel")),
        name="nsa",
    )(qct, qt, gt, kcmp, vcmpt, _overlap_t(seq), ks, vst, kw, vwt)


def _rope_tables(seq):
    inv = 1.0 / (ROPE_THETA ** (jnp.arange(0, HEAD_DIM, 2, dtype=F32) / HEAD_DIM))
    ang = jnp.arange(seq, dtype=F32)[:, None] * inv[None, :]
    cos, sin = jnp.cos(ang), jnp.sin(ang)
    reps = LANE // HEAD_DIM
    cosf = jnp.tile(jnp.concatenate([cos, cos], axis=-1), (1, reps))
    sinf = jnp.tile(jnp.concatenate([-sin, sin], axis=-1), (1, reps))
    return cosf, sinf


def _block_diag(w):
    h, n, _ = w.shape
    eye = jnp.eye(h, dtype=w.dtype)
    return (eye[:, None, :, None] * w[:, :, None, :]).reshape(h * n, h * n)


def _mixers(proj, batch, seq, conv_w, conv_b, wa, ba, wi, bi, lam, pos_k, pos_v, kw1, kw2, vw1, vw2):
    (u, gl, gn, gt, kcr, vcr, qct, qt, ks, vst, kwn, vwt) = proj
    wg = jnp.concatenate([_block_diag(wa), _block_diag(wi)], axis=1).astype(BF16)
    ylru = _lru(u, gl, conv_w, conv_b.reshape(1, D_LRU), wg, ba.reshape(1, D_LRU),
                bi.reshape(1, D_LRU), lam.reshape(1, D_LRU), batch, seq)
    ratio = CMP_LEN // CMP_STRIDE
    kcmp, vcmpt = _compress(
        kcr, vcr, pos_k.reshape(ratio, SUB_FLAT), pos_v.reshape(ratio, SUB_FLAT),
        kw1.reshape(ratio, SUB_FLAT, CMP_HID).astype(BF16), kw2.astype(BF16),
        vw1.reshape(ratio, SUB_FLAT, CMP_HID).astype(BF16), vw2.T.astype(BF16), batch, seq)
    return ylru, gn, _nsa(qct, qt, gt, kcmp, vcmpt, ks, vst, kwn, vwt, batch, seq)


def kernel(x, norm_g, w_in, conv_w, conv_b, lru_wa, lru_ba, lru_wi, lru_bi, lru_lambda, cmp_pos_k, cmp_pos_v, cmp_k_w1, cmp_k_w2, cmp_v_w1, cmp_v_w2, w_out, final_g):
    batch, seq, _ = x.shape
    assert seq % ROW_TILE == 0 and seq % SLC_CHUNK == 0 and (seq // SLC_BLK) % SUBLANE == 0
    cosf, sinf = _rope_tables(seq)
    depth = norm_g.shape[0]
    w_in_p = jnp.pad(w_in, ((0, 0), (0, 0), (0, D_IN_PAD - D_IN))).astype(BF16)
    w_out_b = w_out.astype(BF16)
    h = x.reshape(batch * seq, D_MODEL)
    prev = None
    for l in range(depth):
        proj = _inproj(h, norm_g[l].reshape(1, D_MODEL), w_in_p[l], cosf, sinf, seq, prev)
        if prev is not None:
            h, proj = proj[0], proj[1:]
        ylru, gn, yatt = _mixers(proj, batch, seq, conv_w[l], conv_b[l], lru_wa[l], lru_ba[l], lru_wi[l],
                                 lru_bi[l], lru_lambda[l], cmp_pos_k[l], cmp_pos_v[l], cmp_k_w1[l],
                                 cmp_k_w2[l], cmp_v_w1[l], cmp_v_w2[l])
        prev = (ylru, gn, yatt, w_out_b[l])
    out = _outnorm(h, *prev, final_g.reshape(1, D_MODEL))
    return out.reshape(batch, seq, D_MODEL)
```

```python
import functools

import numpy as np
import jax
import jax.numpy as jnp
from jax import lax
from jax.experimental import pallas as pl
from jax.experimental.pallas import tpu as pltpu

F32 = jnp.float32
BF16 = jnp.bfloat16

D_MODEL = 1024
D_LRU = 512
H_LRU = 8
LRU_BLK = D_LRU // H_LRU
CONV_W = 4
LRU_C = 8.0
D_NSA = 512
HEAD_DIM = 64
N_HEADS = 8
N_KV = 2
GROUP = N_HEADS // N_KV
KV_W = N_KV * HEAD_DIM
CMP_LEN = 32
CMP_STRIDE = 16
CMP_HID = 256
SLC_BLK = 64
SLC_TOPK = 16
WINDOW = 512
ROPE_THETA = 10000.0
EPS = 1e-6
NEG = -1e30
FORCE_BONUS = 1e4
SCALE = HEAD_DIM ** -0.5
LOG2E = 1.4426950408889634
D_IN = 2 * D_LRU + 2 * D_NSA + 6 * KV_W + 3 * N_HEADS
LANE = 128
SUBLANE = 8
D_IN_PAD = ((D_IN + LANE - 1) // LANE) * LANE
SUB_FLAT = CMP_STRIDE * HEAD_DIM
N_BRANCH = 3

OFF_U = 0
OFF_GL = D_LRU
OFF_Q = 2 * D_LRU
OFF_GN = 2 * D_LRU + D_NSA
OFF_KV = 2 * D_LRU + 2 * D_NSA

ROW_TILE = 512
LRU_TILE = 256
ATT_QTILE = 256
SLC_CHUNK = 512
ATT_SUB = 128
PAST_SUB = 256
V_ROWS = 80
VMEM_LIMIT = 48 * 1024 * 1024

_NT = (((1,), (1,)), ((), ()))


def _params(sem):
    return pltpu.CompilerParams(dimension_semantics=sem, vmem_limit_bytes=VMEM_LIMIT)


_sigmoid = jax.nn.sigmoid
_silu = jax.nn.silu


def _rope(t, cos, sin_signed):
    lane = lax.broadcasted_iota(jnp.int32, t.shape, 1)
    first = (lane % HEAD_DIM) < (HEAD_DIM // 2)
    rot = jnp.where(first, pltpu.roll(t, LANE - HEAD_DIM // 2, 1), pltpu.roll(t, HEAD_DIM // 2, 1))
    return t * cos + rot * sin_signed


def _inproj_body(x, g_ref, w_ref, cos_ref, sin_ref,
                 u_ref, gl_ref, gn_ref, gt_ref, kcr_ref, vcr_ref,
                 qct_ref, qt_ref, ks_ref, vst_ref, kw_ref, vwt_ref, kc_scr, vc_scr):
    tm = x.shape[0]
    ms = jnp.mean(x * x, axis=-1, keepdims=True)
    h = ((x * lax.rsqrt(ms + EPS)) * g_ref[...]).astype(BF16)
    cos = cos_ref[...]
    sin = sin_ref[...]

    def proj(off, width):
        return jnp.dot(h, w_ref[:, off:off + width], preferred_element_type=F32)

    u_ref[...] = proj(OFF_U, D_LRU)
    gl_ref[...] = proj(OFF_GL, D_LRU)
    gn_ref[...] = proj(OFF_GN, D_NSA)
    q = proj(OFF_Q, D_NSA)
    kvg = proj(OFF_KV, D_IN_PAD - OFF_KV)
    group = lambda i: kvg[:, i * KV_W:(i + 1) * KV_W]

    def put_heads_t(ref, base, slab_t):
        ref[base, 0:HEAD_DIM, :] = slab_t[:HEAD_DIM].astype(ref.dtype)
        ref[base + 1, 0:HEAD_DIM, :] = slab_t[HEAD_DIM:].astype(ref.dtype)

    for i in range(D_NSA // LANE):
        slab = q[:, i * LANE:(i + 1) * LANE]
        put_heads_t(qct_ref, 2 * i, (slab * (SCALE * LOG2E)).T)
        put_heads_t(qt_ref, 2 * i, (_rope(slab, cos, sin) * (SCALE * LOG2E)).T)

    for scr, dst, i in ((kc_scr, kcr_ref, 0), (vc_scr, vcr_ref, 1)):
        scr[...] = group(i)
        for j in range(CMP_STRIDE):
            rows = scr[pl.ds(j, tm // CMP_STRIDE, stride=CMP_STRIDE), :]
            for hh in range(N_KV):
                dst[hh, :, j * HEAD_DIM:(j + 1) * HEAD_DIM] = rows[:, hh * HEAD_DIM:(hh + 1) * HEAD_DIM]

    ks = _rope(group(2), cos, sin)
    lane = lax.broadcasted_iota(jnp.int32, (tm, LANE), 1)
    blk = (lax.broadcasted_iota(jnp.int32, (tm, LANE), 0) // SLC_BLK) % (SLC_CHUNK // SLC_BLK)
    onehot = (lane - HEAD_DIM == blk).astype(F32)
    ks_ref[0] = jnp.where(lane < HEAD_DIM, ks, onehot).astype(ks_ref.dtype)
    ks_ref[1] = jnp.where(lane < HEAD_DIM, pltpu.roll(ks, HEAD_DIM, 1), onehot).astype(ks_ref.dtype)
    kw = _rope(group(4), cos, sin)
    kw_ref[0] = kw[:, :HEAD_DIM].astype(kw_ref.dtype)
    kw_ref[1] = kw[:, HEAD_DIM:].astype(kw_ref.dtype)

    tail = (lax.broadcasted_iota(jnp.int32, (V_ROWS - HEAD_DIM, tm), 0) == 0).astype(BF16)
    for ref, i in ((vst_ref, 3), (vwt_ref, 5)):
        put_heads_t(ref, 0, group(i).T)
        for hh in range(N_KV):
            ref[hh, HEAD_DIM:V_ROWS, :] = tail

    gate_t = group(6).T
    fill = jnp.zeros((SUBLANE - N_BRANCH, tm), F32)
    for hd in range(N_HEADS):
        gt_ref[hd] = jnp.concatenate([gate_t[N_BRANCH * hd:N_BRANCH * (hd + 1)], fill], axis=0)


def _outproj_rows(x_ref, y_ref, gn_ref, oa_ref, wo_ref):
    y_nsa = oa_ref[...] * _silu(gn_ref[...])
    acc = jnp.dot(y_ref[...], wo_ref[0:D_LRU, :], preferred_element_type=F32)
    acc = acc + jnp.dot(y_nsa.astype(BF16), wo_ref[D_LRU:, :], preferred_element_type=F32)
    return x_ref[...] + acc


def _inproj_kernel(x_ref, *refs):
    _inproj_body(x_ref[...], *refs)


def _outin_kernel(x_ref, y_ref, gn_ref, oa_ref, wo_ref, g_ref, w_ref, cos_ref, sin_ref, xo_ref, *refs):
    x = _outproj_rows(x_ref, y_ref, gn_ref, oa_ref, wo_ref)
    xo_ref[...] = x
    _inproj_body(x, g_ref, w_ref, cos_ref, sin_ref, *refs)


def _outnorm_kernel(x_ref, y_ref, gn_ref, oa_ref, wo_ref, g_ref, o_ref):
    x = _outproj_rows(x_ref, y_ref, gn_ref, oa_ref, wo_ref)
    ms = jnp.mean(x * x, axis=-1, keepdims=True)
    o_ref[...] = (x * lax.rsqrt(ms + EPS)) * g_ref[...]


def _const(shape):
    return pl.BlockSpec(shape, lambda i: (0,) * len(shape), pipeline_mode=pl.Buffered(1))


def _row(width):
    return pl.BlockSpec((ROW_TILE, width), lambda i: (i, 0))


def _outproj_specs():
    return [_row(D_MODEL), _row(D_LRU), _row(D_NSA), _row(D_NSA), _const((D_MODEL, D_MODEL))]


def _outnorm(x2, ylru, gn, oatt, wo, g):
    return pl.pallas_call(
        _outnorm_kernel,
        grid=(x2.shape[0] // ROW_TILE,),
        in_specs=_outproj_specs() + [_const((1, D_MODEL))],
        out_specs=_row(D_MODEL),
        out_shape=jax.ShapeDtypeStruct(x2.shape, F32),
        compiler_params=_params(("parallel",)),
        name="outnorm",
    )(x2, ylru, gn, oatt, wo, g)


def _inproj(x2, g, w, cosf, sinf, seq, prev=None):
    m = x2.shape[0]
    tm = ROW_TILE
    spt = seq // tm
    assert tm % SLC_CHUNK == 0
    row = _row
    heads = lambda n, width: pl.BlockSpec((n, tm, width), lambda i: (0, i, 0))
    heads_t = lambda n, rows: pl.BlockSpec((n, rows, tm), lambda i: (0, 0, i))
    flat = pl.BlockSpec((N_KV, tm // CMP_STRIDE, SUB_FLAT), lambda i: (0, i, 0))
    tab = pl.BlockSpec((tm, LANE), lambda i: (i % spt, 0))
    out_shape = (
        jax.ShapeDtypeStruct((m, D_LRU), F32),
        jax.ShapeDtypeStruct((m, D_LRU), F32),
        jax.ShapeDtypeStruct((m, D_NSA), F32),
        jax.ShapeDtypeStruct((N_HEADS, SUBLANE, m), F32),
        jax.ShapeDtypeStruct((N_KV, m // CMP_STRIDE, SUB_FLAT), F32),
        jax.ShapeDtypeStruct((N_KV, m // CMP_STRIDE, SUB_FLAT), F32),
        jax.ShapeDtypeStruct((N_HEADS, HEAD_DIM, m), BF16),
        jax.ShapeDtypeStruct((N_HEADS, HEAD_DIM, m), BF16),
        jax.ShapeDtypeStruct((N_KV, m, LANE), BF16),
        jax.ShapeDtypeStruct((N_KV, V_ROWS, m), BF16),
        jax.ShapeDtypeStruct((N_KV, m, HEAD_DIM), BF16),
        jax.ShapeDtypeStruct((N_KV, V_ROWS, m), BF16),
    )
    out_specs = (row(D_LRU), row(D_LRU), row(D_NSA), heads_t(N_HEADS, SUBLANE), flat, flat,
                 heads_t(N_HEADS, HEAD_DIM), heads_t(N_HEADS, HEAD_DIM), heads(N_KV, LANE), heads_t(N_KV, V_ROWS),
                 heads(N_KV, HEAD_DIM), heads_t(N_KV, V_ROWS))
    proj_specs = [_const((1, D_MODEL)), _const((D_MODEL, D_IN_PAD)), tab, tab]
    scratch = [pltpu.VMEM((tm, KV_W), F32)] * 2
    if prev is None:
        return pl.pallas_call(
            _inproj_kernel,
            grid=(m // tm,),
            in_specs=[row(D_MODEL)] + proj_specs,
            out_specs=out_specs,
            out_shape=out_shape,
            scratch_shapes=scratch,
            compiler_params=_params(("parallel",)),
            name="inproj",
        )(x2, g, w, cosf, sinf)
    ylru, gn, oatt, wo = prev
    return pl.pallas_call(
        _outin_kernel,
        grid=(m // tm,),
        in_specs=_outproj_specs() + proj_specs,
        out_specs=(row(D_MODEL),) + out_specs,
        out_shape=(jax.ShapeDtypeStruct(x2.shape, F32),) + out_shape,
        scratch_shapes=scratch,
        compiler_params=_params(("parallel",)),
        name="outin",
    )(x2, ylru, gn, oatt, wo, g, w, cosf, sinf)


def _softplus(x):
    return jnp.maximum(x, 0.0) + jnp.log1p(jnp.exp(-jnp.abs(x)))


def _lru_kernel(u_ref, g_ref, cw_ref, cb_ref, wg_ref, ba_ref, bi_ref, lam_ref, y_ref,
                ubuf, hprev, hseq):
    t = u_ref.shape[0]
    head = SUBLANE

    @pl.when(pl.program_id(1) == 0)
    def _():
        ubuf[0:head, :] = jnp.zeros((head, D_LRU), F32)
        hprev[...] = jnp.zeros_like(hprev)

    ubuf[head:head + t, :] = u_ref[...]
    conv = cb_ref[...]
    for k in range(CONV_W):
        conv = conv + cw_ref[k:k + 1, :] * ubuf[pl.ds(head - (CONV_W - 1) + k, t), :]
    ubuf[head - (CONV_W - 1):head, :] = ubuf[head + t - (CONV_W - 1):head + t, :]

    pre = jnp.dot(conv.astype(BF16), wg_ref[...], preferred_element_type=F32)
    r = _sigmoid(pre[:, :D_LRU] + ba_ref[...])
    gi = _sigmoid(pre[:, D_LRU:] + bi_ref[...])
    log_a = (-LRU_C * r) * _softplus(-lam_ref[...])
    a = jnp.exp(log_a)
    drive = jnp.sqrt((1.0 - a) * (1.0 + a)) * (gi * conv)

    ng = t // SUBLANE
    a3 = a.reshape(ng, SUBLANE, D_LRU)
    b3 = drive.reshape(ng, SUBLANE, D_LRU)
    sub = lax.broadcasted_iota(jnp.int32, (ng, SUBLANE, D_LRU), 1)
    d = 1
    while d < SUBLANE:
        keep = sub >= d
        b3 = jnp.where(keep, b3 + a3 * pltpu.roll(b3, d, 1), b3)
        a3 = jnp.where(keep, a3 * pltpu.roll(a3, d, 1), a3)
        d *= 2
    h_in = jnp.broadcast_to(hprev[...], (SUBLANE, D_LRU))
    for gidx in range(ng):
        h_g = a3[gidx] * h_in + b3[gidx]
        hseq[gidx * SUBLANE:(gidx + 1) * SUBLANE, :] = h_g
        h_in = jnp.broadcast_to(h_g[SUBLANE - 1:SUBLANE, :], (SUBLANE, D_LRU))
    hprev[...] = h_in[0:1, :]
    y_ref[...] = (hseq[...] * _silu(g_ref[...])).astype(y_ref.dtype)


def _lru(u, gl, cw, cb, wg, ba, bi, lam, batch, seq):
    t = LRU_TILE
    spt = seq // t
    row = pl.BlockSpec((t, D_LRU), lambda b, j: (b * spt + j, 0))
    vec = pl.BlockSpec((1, D_LRU), lambda b, j: (0, 0))
    return pl.pallas_call(
        _lru_kernel,
        grid=(batch, spt),
        in_specs=[row, row,
                  pl.BlockSpec((CONV_W, D_LRU), lambda b, j: (0, 0)), vec,
                  pl.BlockSpec((D_LRU, 2 * D_LRU), lambda b, j: (0, 0)), vec, vec, vec],
        out_specs=row,
        out_shape=jax.ShapeDtypeStruct(u.shape, BF16),
        scratch_shapes=[pltpu.VMEM((SUBLANE + t, D_LRU), F32), pltpu.VMEM((1, D_LRU), F32),
                        pltpu.VMEM((t, D_LRU), F32)],
        compiler_params=_params(("parallel", "arbitrary")),
        name="rglru",
    )(u, gl, cw, cb, wg, ba, bi, lam)


def _compress_kernel(k_ref, v_ref, pk_ref, pv_ref, k1_ref, k2_ref, v1_ref, v2t_ref, ko_ref, vo_ref, shift):
    n = k_ref.shape[1]

    def hidden(x_ref, p_ref, w1_ref):
        x = x_ref[0]
        lo = jnp.dot((x + p_ref[0:1, :]).astype(BF16), w1_ref[0], preferred_element_type=F32)
        hi = jnp.dot((x + p_ref[1:2, :]).astype(BF16), w1_ref[1], preferred_element_type=F32)
        shift[0:n, :] = hi
        shift[n:n + 8, :] = jnp.zeros((8, CMP_HID), F32)
        return _silu(lo + shift[pl.ds(1, n), :]).astype(BF16)

    ko_ref[0, 0] = jnp.dot(hidden(k_ref, pk_ref, k1_ref), k2_ref[...],
                           preferred_element_type=F32).astype(ko_ref.dtype)
    vo_ref[0, 0] = lax.dot_general(v2t_ref[...], hidden(v_ref, pv_ref, v1_ref), _NT,
                                   preferred_element_type=F32).astype(vo_ref.dtype)


def _compress(kcr, vcr, pk, pv, k1, k2, v1, v2t, batch, seq):
    n = seq // CMP_STRIDE
    xin = pl.BlockSpec((1, n, SUB_FLAT), lambda b, k: (k, b, 0))
    pos = pl.BlockSpec((2, SUB_FLAT), lambda b, k: (0, 0))
    w1 = pl.BlockSpec((2, SUB_FLAT, CMP_HID), lambda b, k: (0, 0, 0))
    return pl.pallas_call(
        _compress_kernel,
        grid=(batch, N_KV),
        in_specs=[xin, xin, pos, pos, w1, pl.BlockSpec((CMP_HID, HEAD_DIM), lambda b, k: (0, 0)),
                  w1, pl.BlockSpec((HEAD_DIM, CMP_HID), lambda b, k: (0, 0))],
        out_specs=(pl.BlockSpec((1, 1, n, HEAD_DIM), lambda b, k: (b, k, 0, 0)),
                   pl.BlockSpec((1, 1, HEAD_DIM, n), lambda b, k: (b, k, 0, 0))),
        out_shape=(jax.ShapeDtypeStruct((batch, N_KV, n, HEAD_DIM), BF16),
                   jax.ShapeDtypeStruct((batch, N_KV, HEAD_DIM, n), BF16)),
        scratch_shapes=[pltpu.VMEM((n + 8, CMP_HID), F32)],
        compiler_params=_params(("parallel", "parallel")),
        name="compress",
    )(kcr, vcr, pk, pv, k1, k2, v1, v2t)


def _attn_queries(qt_ref):
    return [jnp.concatenate([qt_ref[2 * p], qt_ref[2 * p + 1]], axis=1) for p in range(qt_ref.shape[0] // 2)]


def _kv_of(pair):
    return pair // (GROUP // 2)


def _attn_store(o_ref, gt_ref, branches):
    for p in range(len(branches[0])):
        tq = branches[0][p].shape[1] // 2
        halves = []
        for e in range(2):
            cols = slice(e * tq, (e + 1) * tq)
            terms = [_sigmoid(gt_ref[2 * p + e, br:br + 1, :]) * outs[p][:, cols]
                     for br, outs in enumerate(branches)]
            halves.append((terms[0] + terms[1]) + terms[2])
        o_ref[:, p * 2 * HEAD_DIM:(p + 1) * 2 * HEAD_DIM] = jnp.concatenate(halves, axis=0).T


def _attn_pieces(scores, vts, masks, carry):
    state = list(carry)
    for j, vt_j in enumerate(vts):
        for p, (m_i, acc) in enumerate(state):
            vt = vt_j[_kv_of(p)]
            st = scores[j][p]
            if masks[j] is not None:
                st = jnp.where(masks[j], st, NEG)
            m_new = jnp.maximum(m_i, jnp.max(st, axis=0, keepdims=True))
            pt = jnp.exp2(st - m_new).astype(BF16)
            acc = jnp.exp2(m_i - m_new) * acc + jnp.dot(vt, pt, preferred_element_type=F32)
            state[p] = (m_new, acc)
    return tuple(state)


def _attn_init(tq):
    one = (jnp.full((1, 2 * tq), NEG, F32), jnp.zeros((V_ROWS, 2 * tq), F32))
    return (one,) * (N_HEADS // 2)


def _attn_finish(carry):
    return [acc[:HEAD_DIM] / acc[HEAD_DIM:HEAD_DIM + 1] for _, acc in carry]


def _split3(x):
    hi = x.astype(BF16)
    r1 = x - hi.astype(F32)
    mid = r1.astype(BF16)
    lo = (r1 - mid.astype(F32)).astype(BF16)
    return hi, mid, lo


def _cmp_scores(qt_ref, kc_ref):
    return [jnp.dot(kc_ref[0, _kv_of(p)], q, preferred_element_type=F32)
            for p, q in enumerate(_attn_queries(qt_ref))]


def _cmp_attend(scores, vct_ref, mt_ref, t0, tok):
    ncp, tq = scores[0].shape[0], scores[0].shape[1] // 2
    nb = mt_ref.shape[0]
    blk_end = lax.broadcasted_iota(jnp.int32, (ncp, 2 * tq), 0) * CMP_STRIDE + (CMP_LEN - 1)
    valid = blk_end <= tok
    any_valid = tok >= CMP_LEN - 1
    psum = [jnp.zeros((ncp, tq), F32) for _ in range(N_KV)]
    outs = []
    for p, raw in enumerate(scores):
        st = jnp.where(valid, raw, NEG)
        e = jnp.exp2(st - jnp.max(st, axis=0, keepdims=True))
        inv = jnp.where(any_valid, 1.0 / jnp.sum(e, axis=0, keepdims=True), 0.0)
        prob = e * inv
        outs.append(jnp.dot(vct_ref[0, _kv_of(p)], prob.astype(BF16), preferred_element_type=F32))
        psum[_kv_of(p)] = psum[_kv_of(p)] + (prob[:, :tq] + prob[:, tq:])

    mt = mt_ref[...]
    jj = lax.broadcasted_iota(jnp.int32, (nb, tq), 0)
    cur = (t0 + lax.broadcasted_iota(jnp.int32, (nb, tq), 1)) // SLC_BLK
    forced = (jj == 0) | (jj == cur) | (jj == cur - 1)
    block_scores = []
    for total in psum:
        sc = jnp.zeros((nb, tq), F32)
        for part in _split3(total):
            sc = sc + jnp.dot(mt, part, preferred_element_type=F32)
        sc = jnp.where(forced, sc + FORCE_BONUS, sc)
        block_scores.append(jnp.where(jj <= cur, sc, NEG))
    return outs, block_scores


def _select_blocks(block_scores, bias_ref, t0):
    nb, tq = block_scores[0].shape
    ngrp = nb // SUBLANE

    def rank(sc, live):
        rows = [sc[g * SUBLANE:(g + 1) * SUBLANE] for g in range(live)]
        cnt = [jnp.zeros((SUBLANE, tq), jnp.int32) for _ in range(live)]
        sub = lax.broadcasted_iota(jnp.int32, (SUBLANE, tq), 0)
        for gp in range(live):
            for r in range(SUBLANE):
                other = jnp.broadcast_to(rows[gp][r:r + 1], (SUBLANE, tq))
                for g in range(live):
                    if g < gp:
                        before = other > rows[g]
                    elif g > gp:
                        before = other >= rows[g]
                    else:
                        before = (other > rows[g]) | ((other == rows[g]) & (sub > r))
                    cnt[g] = cnt[g] + before.astype(jnp.int32)
        groups = [jnp.where((cnt[g] < min(SLC_TOPK, nb)) & (rows[g] > NEG / 2), 0.0, NEG) for g in range(live)]
        if live < ngrp:
            groups.append(jnp.full(((ngrp - live) * SUBLANE, tq), NEG, F32))
        return jnp.concatenate(groups, axis=0)

    def rank_all(*scs, live):
        return tuple(rank(sc, live) for sc in scs)

    last_group = (t0 + tq - 1) // (SLC_BLK * SUBLANE)
    biases = lax.switch(jnp.minimum(last_group, ngrp - 1),
                        [functools.partial(rank_all, live=k + 1) for k in range(ngrp)], *block_scores)
    for kv, bias in enumerate(biases):
        bias_ref[kv] = bias


def _overlap_t(seq):
    nc_pad = seq // CMP_STRIDE
    n_slc = seq // SLC_BLK
    c0 = np.arange(nc_pad)[None, :] * CMP_STRIDE
    j0 = np.arange(n_slc)[:, None] * SLC_BLK
    ov = np.clip(np.minimum(c0 + CMP_LEN, j0 + SLC_BLK) - np.maximum(c0, j0), 0, None) / CMP_LEN
    ov[:, nc_pad - 1] = 0.0
    return jnp.asarray(ov, dtype=BF16)


def _slc_pieces(qs, k_ref, vt_ref, bias_ref, tok, chunk, first, count, causal, size=ATT_SUB):
    tq = tok.shape[1] // 2
    bpc = SLC_CHUNK // SLC_BLK
    zrows = jnp.zeros((LANE - HEAD_DIM - 2 * bpc, 2 * tq), BF16)
    brows = []
    for kv in range(N_KV):
        b8 = bias_ref[kv, pl.ds(pl.multiple_of(chunk * bpc, bpc), bpc), :]
        b16 = jnp.concatenate([b8, jnp.zeros_like(b8)], axis=0).astype(BF16)
        brows.append(jnp.concatenate([b16, b16], axis=1))
    rhs = [jnp.concatenate([q, brows[_kv_of(p)], zrows], axis=0) for p, q in enumerate(qs)]
    scores, vts, masks = [], [], []
    for j in range(count):
        start = pl.multiple_of(first + j * size, size)
        kcs = [k_ref[kv, pl.ds(start, size), :] for kv in range(N_KV)]
        scores.append([jnp.dot(kcs[_kv_of(p)], r, preferred_element_type=F32) for p, r in enumerate(rhs)])
        vts.append([vt_ref[kv, :, pl.ds(start, size)] for kv in range(N_KV)])
        row = lax.broadcasted_iota(jnp.int32, (size, 2 * tq), 0)
        masks.append(row <= tok - start if causal else None)
    return scores, vts, masks


def _slc_recent(qs, k_ref, vt_ref, bias_ref, i, tok):
    tq = tok.shape[1] // 2
    ck = SLC_CHUNK
    assert ck % tq == 0 and ck // tq == 2
    chunk = (i * tq) // ck
    own = functools.partial(_slc_pieces, qs, k_ref, vt_ref, bias_ref, tok, chunk, i * tq, tq // ATT_SUB, True)

    def own_only():
        return _attn_pieces(*own(), _attn_init(tq))

    def own_and_earlier():
        earlier = _slc_pieces(qs, k_ref, vt_ref, bias_ref, tok, chunk, chunk * ck, tq // ATT_SUB, False)
        return _attn_pieces(*[a + b for a, b in zip(own(), earlier)], _attn_init(tq))

    return lax.cond(i * tq > chunk * ck, own_and_earlier, own_only)


def _slc_past(qs, k_ref, vt_ref, bias_ref, i, tok, carry):
    tq = tok.shape[1] // 2
    ck = SLC_CHUNK
    per = ck // PAST_SUB

    def chunks(c, carry, n):
        scores, vts, masks = [], [], []
        for u in range(n):
            cc = n * c + u if n > 1 else c
            s, v, m = _slc_pieces(qs, k_ref, vt_ref, bias_ref, tok, cc, cc * ck, per, False, PAST_SUB)
            scores, vts, masks = scores + s, vts + v, masks + m
        return _attn_pieces(scores, vts, masks, carry)

    whole = (i * tq) // ck
    carry = lax.fori_loop(0, whole // 2, functools.partial(chunks, n=2), carry)
    return lax.fori_loop(2 * (whole // 2), whole, functools.partial(chunks, n=1), carry)


def _win_pieces(qs, k_ref, vt_ref, n, tok):
    tq = tok.shape[1] // 2
    sub = ATT_SUB
    nback = WINDOW // tq
    scores, vts, masks = [], [], []
    for back in range(nback + 1):
        kb = n - back
        start = pl.multiple_of(jnp.maximum(kb, 0) * tq, tq)
        for j in range(tq // sub):
            kcs = [k_ref[kv, pl.ds(start + j * sub, sub), :] for kv in range(N_KV)]
            scores.append([jnp.dot(kcs[_kv_of(p)], q, preferred_element_type=F32) for p, q in enumerate(qs)])
            vts.append([vt_ref[kv, :, pl.ds(start + j * sub, sub)] for kv in range(N_KV)])
            row = lax.broadcasted_iota(jnp.int32, (sub, 2 * tq), 0)
            rel = tok - (start + j * sub)
            if back == 0:
                masks.append(row <= rel)
            elif back == nback:
                masks.append(row > jnp.where(kb >= 0, rel - WINDOW, sub))
            else:
                masks.append(jnp.broadcast_to(kb >= 0, (sub, 2 * tq)))
    return scores, vts, masks


def _nsa_kernel(qct_ref, qt_ref, gt_ref, kc_ref, vct_ref, mt_ref, ks_ref, vst_ref, kw_ref, vwt_ref,
                o_ref, bias_scr):
    tq = qt_ref.shape[2]
    i = pl.program_id(1)
    tok = i * tq + lax.broadcasted_iota(jnp.int32, (1, 2 * tq), 1) % tq
    qs = _attn_queries(qt_ref)
    cmp_scores = _cmp_scores(qct_ref, kc_ref)
    win = _win_pieces(qs, kw_ref, vwt_ref, i, tok)
    o_cmp, block_scores = _cmp_attend(cmp_scores, vct_ref, mt_ref, i * tq, tok)
    o_win = _attn_finish(_attn_pieces(*win, _attn_init(tq)))
    _select_blocks(block_scores, bias_scr, i * tq)
    slc = _slc_recent(qs, ks_ref, vst_ref, bias_scr, i, tok)
    o_slc = _attn_finish(_slc_past(qs, ks_ref, vst_ref, bias_scr, i, tok, slc))
    _attn_store(o_ref, gt_ref, (o_cmp, o_slc, o_win))


def _nsa(qct, qt, gt, kcmp, vcmpt, ks, vst, kw, vwt, batch, seq):
    tq = ATT_QTILE
    nq = seq // tq
    ncp = seq // CMP_STRIDE
    nb = seq // SLC_BLK
    qspec = pl.BlockSpec((N_HEADS, HEAD_DIM, tq), lambda b, i: (0, 0, b * nq + i))
    vspec = pl.BlockSpec((N_KV, V_ROWS, seq), lambda b, i: (0, 0, b))
    return pl.pallas_call(
        _nsa_kernel,
        grid=(batch, nq),
        in_specs=[qspec, qspec,
                  pl.BlockSpec((N_HEADS, SUBLANE, tq), lambda b, i: (0, 0, b * nq + i)),
                  pl.BlockSpec((1, N_KV, ncp, HEAD_DIM), lambda b, i: (b, 0, 0, 0)),
                  pl.BlockSpec((1, N_KV, HEAD_DIM, ncp), lambda b, i: (b, 0, 0, 0)),
                  pl.BlockSpec((nb, ncp), lambda b, i: (0, 0)),
                  pl.BlockSpec((N_KV, seq, LANE), lambda b, i: (0, b, 0)), vspec,
                  pl.BlockSpec((N_KV, seq, HEAD_DIM), lambda b, i: (0, b, 0)), vspec],
        out_specs=pl.BlockSpec((tq, D_NSA), lambda b, i: (b * nq + i, 0)),
        out_shape=jax.ShapeDtypeStruct((batch * seq, D_NSA), F32),
        scratch_shapes=[pltpu.VMEM((N_KV, nb, tq), F32)],
        compiler_params=_params(("parallel", "parallel")),
        name="nsa",
    )(qct, qt, gt, kcmp, vcmpt, _overlap_t(seq), ks, vst, kw, vwt)


def _rope_tables(seq):
    inv = 1.0 / (ROPE_THETA ** (jnp.arange(0, HEAD_DIM, 2, dtype=F32) / HEAD_DIM))
    ang = jnp.arange(seq, dtype=F32)[:, None] * inv[None, :]
    cos, sin = jnp.cos(ang), jnp.sin(ang)
    reps = LANE // HEAD_DIM
    cosf = jnp.tile(jnp.concatenate([cos, cos], axis=-1), (1, reps))
    sinf = jnp.tile(jnp.concatenate([-sin, sin], axis=-1), (1, reps))
    return cosf, sinf


def _block_diag(w):
    h, n, _ = w.shape
    eye = jnp.eye(h, dtype=w.dtype)
    return (eye[:, None, :, None] * w[:, :, None, :]).reshape(h * n, h * n)


def _mixers(proj, batch, seq, conv_w, conv_b, wa, ba, wi, bi, lam, pos_k, pos_v, kw1, kw2, vw1, vw2):
    (u, gl, gn, gt, kcr, vcr, qct, qt, ks, vst, kwn, vwt) = proj
    wg = jnp.concatenate([_block_diag(wa), _block_diag(wi)], axis=1).astype(BF16)
    ylru = _lru(u, gl, conv_w, conv_b.reshape(1, D_LRU), wg, ba.reshape(1, D_LRU),
                bi.reshape(1, D_LRU), lam.reshape(1, D_LRU), batch, seq)
    ratio = CMP_LEN // CMP_STRIDE
    kcmp, vcmpt = _compress(
        kcr, vcr, pos_k.reshape(ratio, SUB_FLAT), pos_v.reshape(ratio, SUB_FLAT),
        kw1.reshape(ratio, SUB_FLAT, CMP_HID).astype(BF16), kw2.astype(BF16),
        vw1.reshape(ratio, SUB_FLAT, CMP_HID).astype(BF16), vw2.T.astype(BF16), batch, seq)
    return ylru, gn, _nsa(qct, qt, gt, kcmp, vcmpt, ks, vst, kwn, vwt, batch, seq)


def kernel(x, norm_g, w_in, conv_w, conv_b, lru_wa, lru_ba, lru_wi, lru_bi, lru_lambda, cmp_pos_k, cmp_pos_v, cmp_k_w1, cmp_k_w2, cmp_v_w1, cmp_v_w2, w_out, final_g):
    batch, seq, _ = x.shape
    assert seq % ROW_TILE == 0 and seq % SLC_CHUNK == 0 and (seq // SLC_BLK) % SUBLANE == 0
    cosf, sinf = _rope_tables(seq)
    depth = norm_g.shape[0]
    w_in_p = jnp.pad(w_in, ((0, 0), (0, 0), (0, D_IN_PAD - D_IN))).astype(BF16)
    w_out_b = w_out.astype(BF16)
    h = x.reshape(batch * seq, D_MODEL)
    prev = None
    for l in range(depth):
        proj = _inproj(h, norm_g[l].reshape(1, D_MODEL), w_in_p[l], cosf, sinf, seq, prev)
        if prev is not None:
            h, proj = proj[0], proj[1:]
        ylru, gn, yatt = _mixers(proj, batch, seq, conv_w[l], conv_b[l], lru_wa[l], lru_ba[l], lru_wi[l],
                                 lru_bi[l], lru_lambda[l], cmp_pos_k[l], cmp_pos_v[l], cmp_k_w1[l],
                                 cmp_k_w2[l], cmp_v_w1[l], cmp_v_w2[l])
        prev = (ylru, gn, yatt, w_out_b[l])
    out = _outnorm(h, *prev, final_g.reshape(1, D_MODEL))
    return out.reshape(batch, seq, D_MODEL)
```

```python
import functools

import numpy as np
import jax
import jax.numpy as jnp
from jax import lax
from jax.experimental import pallas as pl
from jax.experimental.pallas import tpu as pltpu

F32 = jnp.float32
BF16 = jnp.bfloat16

D_MODEL = 1024
D_LRU = 512
H_LRU = 8
LRU_BLK = D_LRU // H_LRU
CONV_W = 4
LRU_C = 8.0
D_NSA = 512
HEAD_DIM = 64
N_HEADS = 8
N_KV = 2
GROUP = N_HEADS // N_KV
KV_W = N_KV * HEAD_DIM
CMP_LEN = 32
CMP_STRIDE = 16
CMP_HID = 256
SLC_BLK = 64
SLC_TOPK = 16
WINDOW = 512
ROPE_THETA = 10000.0
EPS = 1e-6
NEG = -1e30
FORCE_BONUS = 1e4
SCALE = HEAD_DIM ** -0.5
LOG2E = 1.4426950408889634
D_IN = 2 * D_LRU + 2 * D_NSA + 6 * KV_W + 3 * N_HEADS
LANE = 128
SUBLANE = 8
D_IN_PAD = ((D_IN + LANE - 1) // LANE) * LANE
SUB_FLAT = CMP_STRIDE * HEAD_DIM
N_BRANCH = 3

OFF_U = 0
OFF_GL = D_LRU
OFF_Q = 2 * D_LRU
OFF_GN = 2 * D_LRU + D_NSA
OFF_KV = 2 * D_LRU + 2 * D_NSA

ROW_TILE = 512
LRU_TILE = 512
ATT_QTILE = 256
SLC_CHUNK = 512
ATT_SUB = 128
PAST_SUB = 256
V_ROWS = 80
VMEM_LIMIT = 48 * 1024 * 1024

_NT = (((1,), (1,)), ((), ()))


def _params(sem):
    return pltpu.CompilerParams(dimension_semantics=sem, vmem_limit_bytes=VMEM_LIMIT)


_sigmoid = jax.nn.sigmoid
_silu = jax.nn.silu


def _rope(t, cos, sin_signed):
    lane = lax.broadcasted_iota(jnp.int32, t.shape, 1)
    first = (lane % HEAD_DIM) < (HEAD_DIM // 2)
    rot = jnp.where(first, pltpu.roll(t, LANE - HEAD_DIM // 2, 1), pltpu.roll(t, HEAD_DIM // 2, 1))
    return t * cos + rot * sin_signed


def _inproj_body(x, g_ref, w_ref, cos_ref, sin_ref,
                 u_ref, gl_ref, gn_ref, gt_ref, kcr_ref, vcr_ref,
                 qct_ref, qt_ref, ks_ref, vst_ref, kw_ref, vwt_ref, kc_scr, vc_scr):
    tm = x.shape[0]
    ms = jnp.mean(x * x, axis=-1, keepdims=True)
    h = ((x * lax.rsqrt(ms + EPS)) * g_ref[...]).astype(BF16)
    cos = cos_ref[...]
    sin = sin_ref[...]

    def proj(off, width):
        return jnp.dot(h, w_ref[:, off:off + width], preferred_element_type=F32)

    u_ref[...] = proj(OFF_U, D_LRU)
    gl_ref[...] = proj(OFF_GL, D_LRU)
    gn_ref[...] = proj(OFF_GN, D_NSA)
    q = proj(OFF_Q, D_NSA)
    kvg = proj(OFF_KV, D_IN_PAD - OFF_KV)
    group = lambda i: kvg[:, i * KV_W:(i + 1) * KV_W]

    def put_heads_t(ref, base, slab_t):
        ref[base, 0:HEAD_DIM, :] = slab_t[:HEAD_DIM].astype(ref.dtype)
        ref[base + 1, 0:HEAD_DIM, :] = slab_t[HEAD_DIM:].astype(ref.dtype)

    for i in range(D_NSA // LANE):
        slab = q[:, i * LANE:(i + 1) * LANE]
        put_heads_t(qct_ref, 2 * i, (slab * (SCALE * LOG2E)).T)
        put_heads_t(qt_ref, 2 * i, (_rope(slab, cos, sin) * (SCALE * LOG2E)).T)

    for scr, dst, i in ((kc_scr, kcr_ref, 0), (vc_scr, vcr_ref, 1)):
        scr[...] = group(i)
        for j in range(CMP_STRIDE):
            rows = scr[pl.ds(j, tm // CMP_STRIDE, stride=CMP_STRIDE), :]
            for hh in range(N_KV):
                dst[hh, :, j * HEAD_DIM:(j + 1) * HEAD_DIM] = rows[:, hh * HEAD_DIM:(hh + 1) * HEAD_DIM]

    ks = _rope(group(2), cos, sin)
    lane = lax.broadcasted_iota(jnp.int32, (tm, LANE), 1)
    blk = (lax.broadcasted_iota(jnp.int32, (tm, LANE), 0) // SLC_BLK) % (SLC_CHUNK // SLC_BLK)
    onehot = (lane - HEAD_DIM == blk).astype(F32)
    ks_ref[0] = jnp.where(lane < HEAD_DIM, ks, onehot).astype(ks_ref.dtype)
    ks_ref[1] = jnp.where(lane < HEAD_DIM, pltpu.roll(ks, HEAD_DIM, 1), onehot).astype(ks_ref.dtype)
    kw = _rope(group(4), cos, sin)
    kw_ref[0] = kw[:, :HEAD_DIM].astype(kw_ref.dtype)
    kw_ref[1] = kw[:, HEAD_DIM:].astype(kw_ref.dtype)

    tail = (lax.broadcasted_iota(jnp.int32, (V_ROWS - HEAD_DIM, tm), 0) == 0).astype(BF16)
    for ref, i in ((vst_ref, 3), (vwt_ref, 5)):
        put_heads_t(ref, 0, group(i).T)
        for hh in range(N_KV):
            ref[hh, HEAD_DIM:V_ROWS, :] = tail

    gate_t = group(6).T
    fill = jnp.zeros((SUBLANE - N_BRANCH, tm), F32)
    for hd in range(N_HEADS):
        gt_ref[hd] = jnp.concatenate([gate_t[N_BRANCH * hd:N_BRANCH * (hd + 1)], fill], axis=0)


def _outproj_rows(x_ref, y_ref, gn_ref, oa_ref, wo_ref):
    y_nsa = oa_ref[...] * _silu(gn_ref[...])
    acc = jnp.dot(y_ref[...], wo_ref[0:D_LRU, :], preferred_element_type=F32)
    acc = acc + jnp.dot(y_nsa.astype(BF16), wo_ref[D_LRU:, :], preferred_element_type=F32)
    return x_ref[...] + acc


def _inproj_kernel(x_ref, *refs):
    _inproj_body(x_ref[...], *refs)


def _outin_kernel(x_ref, y_ref, gn_ref, oa_ref, wo_ref, g_ref, w_ref, cos_ref, sin_ref, xo_ref, *refs):
    x = _outproj_rows(x_ref, y_ref, gn_ref, oa_ref, wo_ref)
    xo_ref[...] = x
    _inproj_body(x, g_ref, w_ref, cos_ref, sin_ref, *refs)


def _outnorm_kernel(x_ref, y_ref, gn_ref, oa_ref, wo_ref, g_ref, o_ref):
    x = _outproj_rows(x_ref, y_ref, gn_ref, oa_ref, wo_ref)
    ms = jnp.mean(x * x, axis=-1, keepdims=True)
    o_ref[...] = (x * lax.rsqrt(ms + EPS)) * g_ref[...]


def _const(shape):
    return pl.BlockSpec(shape, lambda i: (0,) * len(shape), pipeline_mode=pl.Buffered(1))


def _row(width):
    return pl.BlockSpec((ROW_TILE, width), lambda i: (i, 0))


def _outproj_specs():
    return [_row(D_MODEL), _row(D_LRU), _row(D_NSA), _row(D_NSA), _const((D_MODEL, D_MODEL))]


def _outnorm(x2, ylru, gn, oatt, wo, g):
    return pl.pallas_call(
        _outnorm_kernel,
        grid=(x2.shape[0] // ROW_TILE,),
        in_specs=_outproj_specs() + [_const((1, D_MODEL))],
        out_specs=_row(D_MODEL),
        out_shape=jax.ShapeDtypeStruct(x2.shape, F32),
        compiler_params=_params(("parallel",)),
        name="outnorm",
    )(x2, ylru, gn, oatt, wo, g)


def _inproj(x2, g, w, cosf, sinf, seq, prev=None):
    m = x2.shape[0]
    tm = ROW_TILE
    spt = seq // tm
    assert tm % SLC_CHUNK == 0
    row = _row
    heads = lambda n, width: pl.BlockSpec((n, tm, width), lambda i: (0, i, 0))
    heads_t = lambda n, rows: pl.BlockSpec((n, rows, tm), lambda i: (0, 0, i))
    flat = pl.BlockSpec((N_KV, tm // CMP_STRIDE, SUB_FLAT), lambda i: (0, i, 0))
    tab = pl.BlockSpec((tm, LANE), lambda i: (i % spt, 0))
    out_shape = (
        jax.ShapeDtypeStruct((m, D_LRU), F32),
        jax.ShapeDtypeStruct((m, D_LRU), F32),
        jax.ShapeDtypeStruct((m, D_NSA), F32),
        jax.ShapeDtypeStruct((N_HEADS, SUBLANE, m), F32),
        jax.ShapeDtypeStruct((N_KV, m // CMP_STRIDE, SUB_FLAT), F32),
        jax.ShapeDtypeStruct((N_KV, m // CMP_STRIDE, SUB_FLAT), F32),
        jax.ShapeDtypeStruct((N_HEADS, HEAD_DIM, m), BF16),
        jax.ShapeDtypeStruct((N_HEADS, HEAD_DIM, m), BF16),
        jax.ShapeDtypeStruct((N_KV, m, LANE), BF16),
        jax.ShapeDtypeStruct((N_KV, V_ROWS, m), BF16),
        jax.ShapeDtypeStruct((N_KV, m, HEAD_DIM), BF16),
        jax.ShapeDtypeStruct((N_KV, V_ROWS, m), BF16),
    )
    out_specs = (row(D_LRU), row(D_LRU), row(D_NSA), heads_t(N_HEADS, SUBLANE), flat, flat,
                 heads_t(N_HEADS, HEAD_DIM), heads_t(N_HEADS, HEAD_DIM), heads(N_KV, LANE), heads_t(N_KV, V_ROWS),
                 heads(N_KV, HEAD_DIM), heads_t(N_KV, V_ROWS))
    proj_specs = [_const((1, D_MODEL)), _const((D_MODEL, D_IN_PAD)), tab, tab]
    scratch = [pltpu.VMEM((tm, KV_W), F32)] * 2
    if prev is None:
        return pl.pallas_call(
            _inproj_kernel,
            grid=(m // tm,),
            in_specs=[row(D_MODEL)] + proj_specs,
            out_specs=out_specs,
            out_shape=out_shape,
            scratch_shapes=scratch,
            compiler_params=_params(("parallel",)),
            name="inproj",
        )(x2, g, w, cosf, sinf)
    ylru, gn, oatt, wo = prev
    return pl.pallas_call(
        _outin_kernel,
        grid=(m // tm,),
        in_specs=_outproj_specs() + proj_specs,
        out_specs=(row(D_MODEL),) + out_specs,
        out_shape=(jax.ShapeDtypeStruct(x2.shape, F32),) + out_shape,
        scratch_shapes=scratch,
        compiler_params=_params(("parallel",)),
        name="outin",
    )(x2, ylru, gn, oatt, wo, g, w, cosf, sinf)


def _softplus(x):
    return jnp.maximum(x, 0.0) + jnp.log1p(jnp.exp(-jnp.abs(x)))


def _lru_kernel(u_ref, g_ref, cw_ref, cb_ref, wg_ref, ba_ref, bi_ref, lam_ref, y_ref,
                ubuf, hprev, hseq):
    t = u_ref.shape[0]
    head = SUBLANE

    @pl.when(pl.program_id(1) == 0)
    def _():
        ubuf[0:head, :] = jnp.zeros((head, D_LRU), F32)
        hprev[...] = jnp.zeros_like(hprev)

    ubuf[head:head + t, :] = u_ref[...]
    conv = cb_ref[...]
    for k in range(CONV_W):
        conv = conv + cw_ref[k:k + 1, :] * ubuf[pl.ds(head - (CONV_W - 1) + k, t), :]
    ubuf[head - (CONV_W - 1):head, :] = ubuf[head + t - (CONV_W - 1):head + t, :]

    pre = jnp.dot(conv.astype(BF16), wg_ref[...], preferred_element_type=F32)
    r = _sigmoid(pre[:, :D_LRU] + ba_ref[...])
    gi = _sigmoid(pre[:, D_LRU:] + bi_ref[...])
    log_a = (-LRU_C * r) * _softplus(-lam_ref[...])
    a = jnp.exp(log_a)
    drive = jnp.sqrt((1.0 - a) * (1.0 + a)) * (gi * conv)

    ng = t // SUBLANE
    a3 = a.reshape(ng, SUBLANE, D_LRU)
    b3 = drive.reshape(ng, SUBLANE, D_LRU)
    sub = lax.broadcasted_iota(jnp.int32, (ng, SUBLANE, D_LRU), 1)
    d = 1
    while d < SUBLANE:
        keep = sub >= d
        b3 = jnp.where(keep, b3 + a3 * pltpu.roll(b3, d, 1), b3)
        a3 = jnp.where(keep, a3 * pltpu.roll(a3, d, 1), a3)
        d *= 2
    h_in = jnp.broadcast_to(hprev[...], (SUBLANE, D_LRU))
    for gidx in range(ng):
        h_g = a3[gidx] * h_in + b3[gidx]
        hseq[gidx * SUBLANE:(gidx + 1) * SUBLANE, :] = h_g
        h_in = jnp.broadcast_to(h_g[SUBLANE - 1:SUBLANE, :], (SUBLANE, D_LRU))
    hprev[...] = h_in[0:1, :]
    y_ref[...] = (hseq[...] * _silu(g_ref[...])).astype(y_ref.dtype)


def _lru(u, gl, cw, cb, wg, ba, bi, lam, batch, seq):
    t = LRU_TILE
    spt = seq // t
    row = pl.BlockSpec((t, D_LRU), lambda b, j: (b * spt + j, 0))
    vec = pl.BlockSpec((1, D_LRU), lambda b, j: (0, 0))
    return pl.pallas_call(
        _lru_kernel,
        grid=(batch, spt),
        in_specs=[row, row,
                  pl.BlockSpec((CONV_W, D_LRU), lambda b, j: (0, 0)), vec,
                  pl.BlockSpec((D_LRU, 2 * D_LRU), lambda b, j: (0, 0)), vec, vec, vec],
        out_specs=row,
        out_shape=jax.ShapeDtypeStruct(u.shape, BF16),
        scratch_shapes=[pltpu.VMEM((SUBLANE + t, D_LRU), F32), pltpu.VMEM((1, D_LRU), F32),
                        pltpu.VMEM((t, D_LRU), F32)],
        compiler_params=_params(("parallel", "arbitrary")),
        name="rglru",
    )(u, gl, cw, cb, wg, ba, bi, lam)


def _compress_kernel(k_ref, v_ref, pk_ref, pv_ref, k1_ref, k2_ref, v1_ref, v2t_ref, ko_ref, vo_ref, shift):
    n = k_ref.shape[1]

    def hidden(x_ref, p_ref, w1_ref):
        x = x_ref[0]
        lo = jnp.dot((x + p_ref[0:1, :]).astype(BF16), w1_ref[0], preferred_element_type=F32)
        hi = jnp.dot((x + p_ref[1:2, :]).astype(BF16), w1_ref[1], preferred_element_type=F32)
        shift[0:n, :] = hi
        shift[n:n + 8, :] = jnp.zeros((8, CMP_HID), F32)
        return _silu(lo + shift[pl.ds(1, n), :]).astype(BF16)

    ko_ref[0, 0] = jnp.dot(hidden(k_ref, pk_ref, k1_ref), k2_ref[...],
                           preferred_element_type=F32).astype(ko_ref.dtype)
    vo_ref[0, 0] = lax.dot_general(v2t_ref[...], hidden(v_ref, pv_ref, v1_ref), _NT,
                                   preferred_element_type=F32).astype(vo_ref.dtype)


def _compress(kcr, vcr, pk, pv, k1, k2, v1, v2t, batch, seq):
    n = seq // CMP_STRIDE
    xin = pl.BlockSpec((1, n, SUB_FLAT), lambda b, k: (k, b, 0))
    pos = pl.BlockSpec((2, SUB_FLAT), lambda b, k: (0, 0))
    w1 = pl.BlockSpec((2, SUB_FLAT, CMP_HID), lambda b, k: (0, 0, 0))
    return pl.pallas_call(
        _compress_kernel,
        grid=(batch, N_KV),
        in_specs=[xin, xin, pos, pos, w1, pl.BlockSpec((CMP_HID, HEAD_DIM), lambda b, k: (0, 0)),
                  w1, pl.BlockSpec((HEAD_DIM, CMP_HID), lambda b, k: (0, 0))],
        out_specs=(pl.BlockSpec((1, 1, n, HEAD_DIM), lambda b, k: (b, k, 0, 0)),
                   pl.BlockSpec((1, 1, HEAD_DIM, n), lambda b, k: (b, k, 0, 0))),
        out_shape=(jax.ShapeDtypeStruct((batch, N_KV, n, HEAD_DIM), BF16),
                   jax.ShapeDtypeStruct((batch, N_KV, HEAD_DIM, n), BF16)),
        scratch_shapes=[pltpu.VMEM((n + 8, CMP_HID), F32)],
        compiler_params=_params(("parallel", "parallel")),
        name="compress",
    )(kcr, vcr, pk, pv, k1, k2, v1, v2t)


def _attn_queries(qt_ref):
    return [jnp.concatenate([qt_ref[2 * p], qt_ref[2 * p + 1]], axis=1) for p in range(qt_ref.shape[0] // 2)]


def _kv_of(pair):
    return pair // (GROUP // 2)


def _attn_store(o_ref, gt_ref, branches):
    for p in range(len(branches[0])):
        tq = branches[0][p].shape[1] // 2
        halves = []
        for e in range(2):
            cols = slice(e * tq, (e + 1) * tq)
            terms = [_sigmoid(gt_ref[2 * p + e, br:br + 1, :]) * outs[p][:, cols]
                     for br, outs in enumerate(branches)]
            halves.append((terms[0] + terms[1]) + terms[2])
        o_ref[:, p * 2 * HEAD_DIM:(p + 1) * 2 * HEAD_DIM] = jnp.concatenate(halves, axis=0).T


def _attn_pieces(scores, vts, masks, carry):
    state = list(carry)
    for j, vt_j in enumerate(vts):
        for p, (m_i, acc) in enumerate(state):
            vt = vt_j[_kv_of(p)]
            st = scores[j][p]
            if masks[j] is not None:
                st = jnp.where(masks[j], st, NEG)
            m_new = jnp.maximum(m_i, jnp.max(st, axis=0, keepdims=True))
            pt = jnp.exp2(st - m_new).astype(BF16)
            acc = jnp.exp2(m_i - m_new) * acc + jnp.dot(vt, pt, preferred_element_type=F32)
            state[p] = (m_new, acc)
    return tuple(state)


def _attn_init(tq):
    one = (jnp.full((1, 2 * tq), NEG, F32), jnp.zeros((V_ROWS, 2 * tq), F32))
    return (one,) * (N_HEADS // 2)


def _attn_finish(carry):
    return [acc[:HEAD_DIM] / acc[HEAD_DIM:HEAD_DIM + 1] for _, acc in carry]


def _split3(x):
    hi = x.astype(BF16)
    r1 = x - hi.astype(F32)
    mid = r1.astype(BF16)
    lo = (r1 - mid.astype(F32)).astype(BF16)
    return hi, mid, lo


def _cmp_scores(qt_ref, kc_ref):
    return [jnp.dot(kc_ref[0, _kv_of(p)], q, preferred_element_type=F32)
            for p, q in enumerate(_attn_queries(qt_ref))]


def _cmp_attend(scores, vct_ref, mt_ref, t0, tok):
    ncp, tq = scores[0].shape[0], scores[0].shape[1] // 2
    nb = mt_ref.shape[0]
    blk_end = lax.broadcasted_iota(jnp.int32, (ncp, 2 * tq), 0) * CMP_STRIDE + (CMP_LEN - 1)
    valid = blk_end <= tok
    any_valid = tok >= CMP_LEN - 1
    psum = [jnp.zeros((ncp, tq), F32) for _ in range(N_KV)]
    outs = []
    for p, raw in enumerate(scores):
        st = jnp.where(valid, raw, NEG)
        e = jnp.exp2(st - jnp.max(st, axis=0, keepdims=True))
        inv = jnp.where(any_valid, 1.0 / jnp.sum(e, axis=0, keepdims=True), 0.0)
        prob = e * inv
        outs.append(jnp.dot(vct_ref[0, _kv_of(p)], prob.astype(BF16), preferred_element_type=F32))
        psum[_kv_of(p)] = psum[_kv_of(p)] + (prob[:, :tq] + prob[:, tq:])

    mt = mt_ref[...]
    jj = lax.broadcasted_iota(jnp.int32, (nb, tq), 0)
    cur = (t0 + lax.broadcasted_iota(jnp.int32, (nb, tq), 1)) // SLC_BLK
    forced = (jj == 0) | (jj == cur) | (jj == cur - 1)
    block_scores = []
    for total in psum:
        sc = jnp.zeros((nb, tq), F32)
        for part in _split3(total):
            sc = sc + jnp.dot(mt, part, preferred_element_type=F32)
        sc = jnp.where(forced, sc + FORCE_BONUS, sc)
        block_scores.append(jnp.where(jj <= cur, sc, NEG))
    return outs, block_scores


def _select_blocks(block_scores, bias_ref, t0):
    nb, tq = block_scores[0].shape
    ngrp = nb // SUBLANE

    def rank(sc, live):
        rows = [sc[g * SUBLANE:(g + 1) * SUBLANE] for g in range(live)]
        cnt = [jnp.zeros((SUBLANE, tq), jnp.int32) for _ in range(live)]
        sub = lax.broadcasted_iota(jnp.int32, (SUBLANE, tq), 0)
        for gp in range(live):
            for r in range(SUBLANE):
                other = jnp.broadcast_to(rows[gp][r:r + 1], (SUBLANE, tq))
                for g in range(live):
                    if g < gp:
                        before = other > rows[g]
                    elif g > gp:
                        before = other >= rows[g]
                    else:
                        before = (other > rows[g]) | ((other == rows[g]) & (sub > r))
                    cnt[g] = cnt[g] + before.astype(jnp.int32)
        groups = [jnp.where((cnt[g] < min(SLC_TOPK, nb)) & (rows[g] > NEG / 2), 0.0, NEG) for g in range(live)]
        if live < ngrp:
            groups.append(jnp.full(((ngrp - live) * SUBLANE, tq), NEG, F32))
        return jnp.concatenate(groups, axis=0)

    def rank_all(*scs, live):
        return tuple(rank(sc, live) for sc in scs)

    last_group = (t0 + tq - 1) // (SLC_BLK * SUBLANE)
    biases = lax.switch(jnp.minimum(last_group, ngrp - 1),
                        [functools.partial(rank_all, live=k + 1) for k in range(ngrp)], *block_scores)
    for kv, bias in enumerate(biases):
        bias_ref[kv] = bias


def _overlap_t(seq):
    nc_pad = seq // CMP_STRIDE
    n_slc = seq // SLC_BLK
    c0 = np.arange(nc_pad)[None, :] * CMP_STRIDE
    j0 = np.arange(n_slc)[:, None] * SLC_BLK
    ov = np.clip(np.minimum(c0 + CMP_LEN, j0 + SLC_BLK) - np.maximum(c0, j0), 0, None) / CMP_LEN
    ov[:, nc_pad - 1] = 0.0
    return jnp.asarray(ov, dtype=BF16)


def _slc_pieces(qs, k_ref, vt_ref, bias_ref, tok, chunk, first, count, causal, size=ATT_SUB):
    tq = tok.shape[1] // 2
    bpc = SLC_CHUNK // SLC_BLK
    zrows = jnp.zeros((LANE - HEAD_DIM - 2 * bpc, 2 * tq), BF16)
    brows = []
    for kv in range(N_KV):
        b8 = bias_ref[kv, pl.ds(pl.multiple_of(chunk * bpc, bpc), bpc), :]
        b16 = jnp.concatenate([b8, jnp.zeros_like(b8)], axis=0).astype(BF16)
        brows.append(jnp.concatenate([b16, b16], axis=1))
    rhs = [jnp.concatenate([q, brows[_kv_of(p)], zrows], axis=0) for p, q in enumerate(qs)]
    scores, vts, masks = [], [], []
    for j in range(count):
        start = pl.multiple_of(first + j * size, size)
        kcs = [k_ref[kv, pl.ds(start, size), :] for kv in range(N_KV)]
        scores.append([jnp.dot(kcs[_kv_of(p)], r, preferred_element_type=F32) for p, r in enumerate(rhs)])
        vts.append([vt_ref[kv, :, pl.ds(start, size)] for kv in range(N_KV)])
        row = lax.broadcasted_iota(jnp.int32, (size, 2 * tq), 0)
        masks.append(row <= tok - start if causal else None)
    return scores, vts, masks


def _slc_recent(qs, k_ref, vt_ref, bias_ref, i, tok):
    tq = tok.shape[1] // 2
    ck = SLC_CHUNK
    assert ck % tq == 0 and ck // tq == 2
    chunk = (i * tq) // ck
    own = functools.partial(_slc_pieces, qs, k_ref, vt_ref, bias_ref, tok, chunk, i * tq, tq // ATT_SUB, True)

    def own_only():
        return _attn_pieces(*own(), _attn_init(tq))

    def own_and_earlier():
        earlier = _slc_pieces(qs, k_ref, vt_ref, bias_ref, tok, chunk, chunk * ck, tq // ATT_SUB, False)
        return _attn_pieces(*[a + b for a, b in zip(own(), earlier)], _attn_init(tq))

    return lax.cond(i * tq > chunk * ck, own_and_earlier, own_only)


def _slc_past(qs, k_ref, vt_ref, bias_ref, i, tok, carry):
    tq = tok.shape[1] // 2
    ck = SLC_CHUNK
    per = ck // PAST_SUB

    def chunks(c, carry, n):
        scores, vts, masks = [], [], []
        for u in range(n):
            cc = n * c + u if n > 1 else c
            s, v, m = _slc_pieces(qs, k_ref, vt_ref, bias_ref, tok, cc, cc * ck, per, False, PAST_SUB)
            scores, vts, masks = scores + s, vts + v, masks + m
        return _attn_pieces(scores, vts, masks, carry)

    whole = (i * tq) // ck
    carry = lax.fori_loop(0, whole // 2, functools.partial(chunks, n=2), carry)
    return lax.fori_loop(2 * (whole // 2), whole, functools.partial(chunks, n=1), carry)


def _win_pieces(qs, k_ref, vt_ref, n, tok):
    tq = tok.shape[1] // 2
    sub = ATT_SUB
    nback = WINDOW // tq
    scores, vts, masks = [], [], []
    for back in range(nback + 1):
        kb = n - back
        start = pl.multiple_of(jnp.maximum(kb, 0) * tq, tq)
        for j in range(tq // sub):
            kcs = [k_ref[kv, pl.ds(start + j * sub, sub), :] for kv in range(N_KV)]
            scores.append([jnp.dot(kcs[_kv_of(p)], q, preferred_element_type=F32) for p, q in enumerate(qs)])
            vts.append([vt_ref[kv, :, pl.ds(start + j * sub, sub)] for kv in range(N_KV)])
            row = lax.broadcasted_iota(jnp.int32, (sub, 2 * tq), 0)
            rel = tok - (start + j * sub)
            if back == 0:
                masks.append(row <= rel)
            elif back == nback:
                masks.append(row > jnp.where(kb >= 0, rel - WINDOW, sub))
            else:
                masks.append(jnp.broadcast_to(kb >= 0, (sub, 2 * tq)))
    return scores, vts, masks


def _nsa_kernel(qct_ref, qt_ref, gt_ref, kc_ref, vct_ref, mt_ref, ks_ref, vst_ref, kw_ref, vwt_ref,
                o_ref, bias_scr):
    tq = qt_ref.shape[2]
    i = pl.program_id(1)
    tok = i * tq + lax.broadcasted_iota(jnp.int32, (1, 2 * tq), 1) % tq
    qs = _attn_queries(qt_ref)
    cmp_scores = _cmp_scores(qct_ref, kc_ref)
    win = _win_pieces(qs, kw_ref, vwt_ref, i, tok)
    o_cmp, block_scores = _cmp_attend(cmp_scores, vct_ref, mt_ref, i * tq, tok)
    o_win = _attn_finish(_attn_pieces(*win, _attn_init(tq)))
    _select_blocks(block_scores, bias_scr, i * tq)
    slc = _slc_recent(qs, ks_ref, vst_ref, bias_scr, i, tok)
    o_slc = _attn_finish(_slc_past(qs, ks_ref, vst_ref, bias_scr, i, tok, slc))
    _attn_store(o_ref, gt_ref, (o_cmp, o_slc, o_win))


def _nsa(qct, qt, gt, kcmp, vcmpt, ks, vst, kw, vwt, batch, seq):
    tq = ATT_QTILE
    nq = seq // tq
    ncp = seq // CMP_STRIDE
    nb = seq // SLC_BLK
    qspec = pl.BlockSpec((N_HEADS, HEAD_DIM, tq), lambda b, i: (0, 0, b * nq + i))
    vspec = pl.BlockSpec((N_KV, V_ROWS, seq), lambda b, i: (0, 0, b))
    return pl.pallas_call(
        _nsa_kernel,
        grid=(batch, nq),
        in_specs=[qspec, qspec,
                  pl.BlockSpec((N_HEADS, SUBLANE, tq), lambda b, i: (0, 0, b * nq + i)),
                  pl.BlockSpec((1, N_KV, ncp, HEAD_DIM), lambda b, i: (b, 0, 0, 0)),
                  pl.BlockSpec((1, N_KV, HEAD_DIM, ncp), lambda b, i: (b, 0, 0, 0)),
                  pl.BlockSpec((nb, ncp), lambda b, i: (0, 0)),
                  pl.BlockSpec((N_KV, seq, LANE), lambda b, i: (0, b, 0)), vspec,
                  pl.BlockSpec((N_KV, seq, HEAD_DIM), lambda b, i: (0, b, 0)), vspec],
        out_specs=pl.BlockSpec((tq, D_NSA), lambda b, i: (b * nq + i, 0)),
        out_shape=jax.ShapeDtypeStruct((batch * seq, D_NSA), F32),
        scratch_shapes=[pltpu.VMEM((N_KV, nb, tq), F32)],
        compiler_params=_params(("parallel", "parallel")),
        name="nsa",
    )(qct, qt, gt, kcmp, vcmpt, _overlap_t(seq), ks, vst, kw, vwt)


def _rope_tables(seq):
    inv = 1.0 / (ROPE_THETA ** (jnp.arange(0, HEAD_DIM, 2, dtype=F32) / HEAD_DIM))
    ang = jnp.arange(seq, dtype=F32)[:, None] * inv[None, :]
    cos, sin = jnp.cos(ang), jnp.sin(ang)
    reps = LANE // HEAD_DIM
    cosf = jnp.tile(jnp.concatenate([cos, cos], axis=-1), (1, reps))
    sinf = jnp.tile(jnp.concatenate([-sin, sin], axis=-1), (1, reps))
    return cosf, sinf


def _block_diag(w):
    h, n, _ = w.shape
    eye = jnp.eye(h, dtype=w.dtype)
    return (eye[:, None, :, None] * w[:, :, None, :]).reshape(h * n, h * n)


def _mixers(proj, batch, seq, conv_w, conv_b, wa, ba, wi, bi, lam, pos_k, pos_v, kw1, kw2, vw1, vw2):
    (u, gl, gn, gt, kcr, vcr, qct, qt, ks, vst, kwn, vwt) = proj
    wg = jnp.concatenate([_block_diag(wa), _block_diag(wi)], axis=1).astype(BF16)
    ylru = _lru(u, gl, conv_w, conv_b.reshape(1, D_LRU), wg, ba.reshape(1, D_LRU),
                bi.reshape(1, D_LRU), lam.reshape(1, D_LRU), batch, seq)
    ratio = CMP_LEN // CMP_STRIDE
    kcmp, vcmpt = _compress(
        kcr, vcr, pos_k.reshape(ratio, SUB_FLAT), pos_v.reshape(ratio, SUB_FLAT),
        kw1.reshape(ratio, SUB_FLAT, CMP_HID).astype(BF16), kw2.astype(BF16),
        vw1.reshape(ratio, SUB_FLAT, CMP_HID).astype(BF16), vw2.T.astype(BF16), batch, seq)
    return ylru, gn, _nsa(qct, qt, gt, kcmp, vcmpt, ks, vst, kwn, vwt, batch, seq)


def kernel(x, norm_g, w_in, conv_w, conv_b, lru_wa, lru_ba, lru_wi, lru_bi, lru_lambda, cmp_pos_k, cmp_pos_v, cmp_k_w1, cmp_k_w2, cmp_v_w1, cmp_v_w2, w_out, final_g):
    batch, seq, _ = x.shape
    assert seq % ROW_TILE == 0 and seq % SLC_CHUNK == 0 and (seq // SLC_BLK) % SUBLANE == 0
    cosf, sinf = _rope_tables(seq)
    depth = norm_g.shape[0]
    w_in_p = jnp.pad(w_in, ((0, 0), (0, 0), (0, D_IN_PAD - D_IN))).astype(BF16)
    w_out_b = w_out.astype(BF16)
    h = x.reshape(batch * seq, D_MODEL)
    prev = None
    for l in range(depth):
        proj = _inproj(h, norm_g[l].reshape(1, D_MODEL), w_in_p[l], cosf, sinf, seq, prev)
        if prev is not None:
            h, proj = proj[0], proj[1:]
        ylru, gn, yatt = _mixers(proj, batch, seq, conv_w[l], conv_b[l], lru_wa[l], lru_ba[l], lru_wi[l],
                                 lru_bi[l], lru_lambda[l], cmp_pos_k[l], cmp_pos_v[l], cmp_k_w1[l],
                                 cmp_k_w2[l], cmp_v_w1[l], cmp_v_w2[l])
        prev = (ylru, gn, yatt, w_out_b[l])
    out = _outnorm(h, *prev, final_g.reshape(1, D_MODEL))
    return out.reshape(batch, seq, D_MODEL)
```
